```python
import math
import jax, jax.numpy as jnp
from jax import lax
import numpy as np

D_MODEL = 2048
BATCH = 8
SEQ = 2048
DEPTH = 2

N_MIXERS = 2
NORM_EPS = 1e-6
HGRN_HEADS = 16
HGRN_KEY_DIM = D_MODEL // HGRN_HEADS
HGRN_VAL_DIM = D_MODEL // HGRN_HEADS
HGRN_CHUNK = 16
DIFF_HEADS = 8
DIFF_HEAD_DIM = D_MODEL // (2 * DIFF_HEADS)
ROPE_THETA = 10000.0
Q_BLOCK = 128
N_GROUPS = 8
EXPERTS_PER_GROUP = 8
N_EXPERTS = N_GROUPS * EXPERTS_PER_GROUP
TOP_K = 2
EXPERT_FF = D_MODEL // 4
EXPERT_BLOCK = 128

kernel_name = 'hgrn2_diffattn_hier_moe_trunk'


def rms_norm(x, gain):
    xf = x.astype(jnp.float32)
    y = xf * lax.rsqrt(jnp.mean(xf * xf, axis=-1, keepdims=True) + NORM_EPS)
    return (y * gain.astype(jnp.float32)).astype(x.dtype)


def rope(x, positions):
    half = x.shape[-1] // 2
    inv_freq = ROPE_THETA ** (-jnp.arange(half, dtype=jnp.float32) / half)
    ang = positions.astype(jnp.float32)[..., None] * inv_freq
    cos = jnp.cos(ang)[:, :, None, :].astype(x.dtype)
    sin = jnp.sin(ang)[:, :, None, :].astype(x.dtype)
    x1, x2 = x[..., :half], x[..., half:]
    return jnp.concatenate([x1 * cos - x2 * sin, x2 * cos + x1 * sin], axis=-1)


def chunked_gated_recurrence(q, k, v, log_f):
    b, s, h, dk = q.shape
    dv = v.shape[-1]
    nc = s // HGRN_CHUNK
    r = lambda t: t.reshape(b, nc, HGRN_CHUNK, h, t.shape[-1])
    q, k, v, log_f = r(q), r(k), r(v), r(log_f)
    cum = jnp.cumsum(log_f, axis=2)
    total = cum[:, :, -1]
    q_dec = q * jnp.exp(cum)
    k_inv = k * jnp.exp(-cum)
    k_end = k * jnp.exp(total[:, :, None] - cum)
    causal = jnp.tril(jnp.ones((HGRN_CHUNK, HGRN_CHUNK), jnp.float32))
    attn = jnp.einsum('bnchk,bndhk->bnhcd', q_dec, k_inv) * causal
    o_intra = jnp.einsum('bnhcd,bndhv->bnchv', attn, v)

    def step(state, inp):
        q_c, k_c, v_c, tot = inp
        o = jnp.einsum('bchk,bhkv->bchv', q_c, state)
        state = state * jnp.exp(tot)[..., None] + jnp.einsum('bchk,bchv->bhkv', k_c, v_c)
        return state, o

    xs = (jnp.moveaxis(q_dec, 1, 0), jnp.moveaxis(k_end, 1, 0),
          jnp.moveaxis(v, 1, 0), jnp.moveaxis(total, 1, 0))
    _, o_inter = lax.scan(step, jnp.zeros((b, h, dk, dv), jnp.float32), xs)
    o = o_intra + jnp.moveaxis(o_inter, 0, 1)
    return o.reshape(b, s, h, dv)


def hgrn2_mixer(h, w_in, out_gain, w_out, lower_bound):
    b, s, _ = h.shape
    proj = h @ w_in
    q, f, i, g = jnp.split(proj, 4, axis=-1)
    q = jax.nn.silu(q.astype(jnp.float32)).reshape(b, s, HGRN_HEADS, HGRN_KEY_DIM)
    lb = lower_bound.astype(jnp.float32).reshape(HGRN_HEADS, HGRN_KEY_DIM)
    forget = lb + (1.0 - lb) * jax.nn.sigmoid(f.astype(jnp.float32).reshape(b, s, HGRN_HEADS, HGRN_KEY_DIM))
    inp = i.astype(jnp.float32).reshape(b, s, HGRN_HEADS, HGRN_VAL_DIM)
    o = chunked_gated_recurrence(q, 1.0 - forget, inp, jnp.log(forget))
    gate = jax.nn.silu(g.astype(jnp.float32).reshape(b, s, HGRN_HEADS, HGRN_VAL_DIM))
    o = rms_norm(o, out_gain) * gate
    return o.reshape(b, s, -1).astype(h.dtype) @ w_out


def diff_attention(h, positions, w_in, q_gain, k_gain, lam, sub_gain, w_out, lambda_init):
    b, s, _ = h.shape
    nh, hd = DIFF_HEADS, DIFF_HEAD_DIM
    proj = h @ w_in
    q, k, v = jnp.split(proj, [2 * nh * hd, 4 * nh * hd], axis=-1)
    q = rope(rms_norm(q.reshape(b, s, 2 * nh, hd), q_gain), positions)
    k = rope(rms_norm(k.reshape(b, s, 2 * nh, hd), k_gain), positions)
    q = q.transpose(0, 2, 1, 3) * (hd ** -0.5)
    k = k.transpose(0, 2, 1, 3)
    v = v.reshape(b, s, nh, 2 * hd).transpose(0, 2, 1, 3)
    lf = lam.astype(jnp.float32)
    lam_full = jnp.exp(jnp.sum(lf[0] * lf[1])) - jnp.exp(jnp.sum(lf[2] * lf[3])) + lambda_init
    outs = []
    for blk in range(s // Q_BLOCK):
        q0, q1 = blk * Q_BLOCK, (blk + 1) * Q_BLOCK
        scores = jnp.einsum('bhqd,bhkd->bhqk', q[:, :, q0:q1], k[:, :, :q1]).astype(jnp.float32)
        causal = jnp.arange(q1)[None, :] <= jnp.arange(q0, q1)[:, None]
        scores = jnp.where(causal, scores, -jnp.inf)
        p = jax.nn.softmax(scores, axis=-1).reshape(b, nh, 2, Q_BLOCK, q1)
        weights = (p[:, :, 0] - lam_full * p[:, :, 1]).astype(v.dtype)
        outs.append(jnp.einsum('bhqk,bhkv->bhqv', weights, v[:, :, :q1]))
    o = jnp.concatenate(outs, axis=2).transpose(0, 2, 1, 3)
    o = rms_norm(o, sub_gain) * (1.0 - lambda_init)
    return o.reshape(b, s, -1) @ w_out


def grouped_expert_mlp(xt, expert, gate, w_gate, w_up, w_down):
    n, d = xt.shape
    n_assign = n * TOP_K
    flat_e = expert.reshape(-1)
    flat_tok = jnp.arange(n_assign, dtype=jnp.int32) // TOP_K
    flat_w = gate.reshape(-1)
    order = jnp.argsort(flat_e)
    se, stok, sw = flat_e[order], flat_tok[order], flat_w[order]
    counts = jnp.bincount(flat_e, length=N_EXPERTS)
    pad_counts = (counts + EXPERT_BLOCK - 1) // EXPERT_BLOCK * EXPERT_BLOCK
    pad_end = jnp.cumsum(pad_counts)
    pad_start = pad_end - pad_counts
    start = jnp.cumsum(counts) - counts
    slot = pad_start[se] + jnp.arange(n_assign) - start[se]
    n_blocks = -(-n_assign // EXPERT_BLOCK) + N_EXPERTS
    n_rows = n_blocks * EXPERT_BLOCK
    row_tok = jnp.full((n_rows,), n, jnp.int32).at[slot].set(stok)
    row_w = jnp.zeros((n_rows,), jnp.float32).at[slot].set(sw)
    block_expert = jnp.minimum(
        jnp.searchsorted(pad_end, jnp.arange(n_blocks) * EXPERT_BLOCK, side='right'), N_EXPERTS - 1)
    x_pad = jnp.concatenate([xt, jnp.zeros((1, d), xt.dtype)], axis=0)

    def one_block(args):
        tok, e = args
        xb = x_pad[tok]
        hb = jax.nn.silu(xb @ w_gate[e]) * (xb @ w_up[e])
        return hb @ w_down[e]

    yb = lax.map(one_block, (row_tok.reshape(n_blocks, EXPERT_BLOCK), block_expert))
    y = jnp.zeros((n + 1, d), xt.dtype).at[row_tok].add(
        yb.reshape(n_rows, d) * row_w[:, None].astype(xt.dtype))
    return y[:n]


def hier_moe(h, w_group, w_expert, w_gate, w_up, w_down):
    b, s, d = h.shape
    xt = h.reshape(-1, d)
    n = xt.shape[0]
    g_prob = jax.nn.softmax((xt @ w_group).astype(jnp.float32), axis=-1)
    g_top_p, g_top = lax.top_k(g_prob, 1)
    e_logits = (xt @ w_expert).astype(jnp.float32).reshape(n, N_GROUPS, EXPERTS_PER_GROUP)
    e_logits = jnp.take_along_axis(e_logits, g_top[:, :, None], axis=1)[:, 0]
    e_top_p, e_top = lax.top_k(jax.nn.softmax(e_logits, axis=-1), TOP_K)
    gate = g_top_p * e_top_p / jnp.sum(e_top_p, axis=-1, keepdims=True)
    expert = g_top * EXPERTS_PER_GROUP + e_top
    y = grouped_expert_mlp(xt, expert, gate, w_gate, w_up, w_down)
    return y.reshape(b, s, d)


def setup_inputs(seed: int = 0) -> dict:
    key = jax.random.key(seed)
    ks = jax.random.split(key, 20)
    n_a = (DEPTH + N_MIXERS - 1) // N_MIXERS
    n_b = DEPTH // N_MIXERS
    nrm = lambda k, shape, scale: jax.random.normal(k, shape, jnp.float32) * scale
    d = D_MODEL
    return {
        'x': nrm(ks[0], (BATCH, SEQ, d), 1.0),
        'positions': jnp.tile(jnp.arange(SEQ, dtype=jnp.int32)[None, :], (BATCH, 1)),
        'mix_norm': 1.0 + nrm(ks[1], (DEPTH, d), 0.02),
        'ffn_norm': 1.0 + nrm(ks[2], (DEPTH, d), 0.02),
        'hgrn_w_in': nrm(ks[3], (n_a, d, 4 * d), d ** -0.5),
        'hgrn_out_gain': 1.0 + nrm(ks[4], (n_a, HGRN_VAL_DIM), 0.02),
        'hgrn_w_out': nrm(ks[5], (n_a, d, d), d ** -0.5),
        'hgrn_lb_logits': nrm(ks[6], (DEPTH + 1, HGRN_HEADS * HGRN_KEY_DIM), 0.1),
        'diff_w_in': nrm(ks[7], (n_b, d, 6 * DIFF_HEADS * DIFF_HEAD_DIM), d ** -0.5),
        'diff_q_gain': 1.0 + nrm(ks[8], (n_b, DIFF_HEAD_DIM), 0.02),
        'diff_k_gain': 1.0 + nrm(ks[9], (n_b, DIFF_HEAD_DIM), 0.02),
        'diff_lambda': nrm(ks[10], (n_b, 4, DIFF_HEAD_DIM), 0.1),
        'diff_sub_gain': 1.0 + nrm(ks[11], (n_b, 2 * DIFF_HEAD_DIM), 0.02),
        'diff_w_out': nrm(ks[12], (n_b, 2 * DIFF_HEADS * DIFF_HEAD_DIM, d), d ** -0.5),
        'moe_w_group': nrm(ks[13], (DEPTH, d, N_GROUPS), d ** -0.5),
        'moe_w_expert': nrm(ks[14], (DEPTH, d, N_EXPERTS), d ** -0.5),
        'moe_w_gate': nrm(ks[15], (DEPTH, N_EXPERTS, d, EXPERT_FF), d ** -0.5),
        'moe_w_up': nrm(ks[16], (DEPTH, N_EXPERTS, d, EXPERT_FF), d ** -0.5),
        'moe_w_down': nrm(ks[17], (DEPTH, N_EXPERTS, EXPERT_FF, d), EXPERT_FF ** -0.5),
    }


def reference(x, positions, mix_norm, ffn_norm, hgrn_w_in, hgrn_out_gain, hgrn_w_out, hgrn_lb_logits,
              diff_w_in, diff_q_gain, diff_k_gain, diff_lambda, diff_sub_gain, diff_w_out,
              moe_w_group, moe_w_expert, moe_w_gate, moe_w_up, moe_w_down):
    lower_bounds = jnp.cumsum(jax.nn.softmax(hgrn_lb_logits.astype(jnp.float32), axis=0), axis=0)
    h = x
    for layer in range(DEPTH):
        j = layer // N_MIXERS
        hn = rms_norm(h, mix_norm[layer])
        if layer % N_MIXERS == 0:
            mixed = hgrn2_mixer(hn, hgrn_w_in[j], hgrn_out_gain[j], hgrn_w_out[j], lower_bounds[layer])
        else:
            lambda_init = 0.8 - 0.6 * math.exp(-0.3 * layer)
            mixed = diff_attention(hn, positions, diff_w_in[j], diff_q_gain[j], diff_k_gain[j],
                                   diff_lambda[j], diff_sub_gain[j], diff_w_out[j], lambda_init)
        h = h + mixed
        h = h + hier_moe(rms_norm(h, ffn_norm[layer]), moe_w_group[layer], moe_w_expert[layer],
                         moe_w_gate[layer], moe_w_up[layer], moe_w_down[layer])
    return h
```

```python
import functools
import math

import jax
import jax.numpy as jnp
import numpy as np
from jax import lax
from jax.experimental import pallas as pl
from jax.experimental.pallas import tpu as pltpu

F32 = jnp.float32
BF16 = jnp.bfloat16

NORM_EPS = 1e-6
N_MIXERS = 2
HGRN_HEADS = 16
HGRN_DIM = 128
HGRN_BLOCK = 128
HGRN_SUB = 16
DIFF_HEADS = 8
DIFF_HEAD_DIM = 128
ROPE_THETA = 10000.0
ATTN_TILE = 256
N_GROUPS = 8
EXPERTS_PER_GROUP = 8
N_EXPERTS = N_GROUPS * EXPERTS_PER_GROUP
TOP_K = 2
EXPERT_ROWS = 256
LANES = 128
VMEM_LIMIT_BYTES = 56 * 1024 * 1024


def _params(*semantics):
    return pltpu.CompilerParams(dimension_semantics=semantics, vmem_limit_bytes=VMEM_LIMIT_BYTES)


def _norm_kernel(h_ref, g_ref, o_ref):
    x = h_ref[...]
    ms = jnp.mean(x * x, axis=-1, keepdims=True)
    o_ref[...] = (x * lax.rsqrt(ms + NORM_EPS) * g_ref[...]).astype(o_ref.dtype)


def _rmsnorm_bf16(h, gain, tm=512):
    n, d = h.shape
    return pl.pallas_call(
        _norm_kernel,
        grid=(n // tm,),
        in_specs=[pl.BlockSpec((tm, d), lambda i: (i, 0)), pl.BlockSpec((1, d), lambda i: (0, 0))],
        out_specs=pl.BlockSpec((tm, d), lambda i: (i, 0)),
        out_shape=jax.ShapeDtypeStruct((n, d), BF16),
        compiler_params=_params("parallel"),
        name="rmsnorm_bf16",
    )(h, gain.reshape(1, d).astype(F32))


def _mm_res_kernel(a_ref, w_ref, r_ref, o_ref):
    o_ref[...] = r_ref[...] + jnp.dot(a_ref[...], w_ref[...], preferred_element_type=F32)


def _matmul_residual(a, w, res, tm=1024, tn=1024):
    n, k = a.shape
    nout = w.shape[1]
    tm, tn = min(tm, n), min(tn, nout)
    return pl.pallas_call(
        _mm_res_kernel,
        grid=(n // tm, nout // tn),
        in_specs=[pl.BlockSpec((tm, k), lambda i, j: (i, 0)),
                  pl.BlockSpec((k, tn), lambda i, j: (0, j)),
                  pl.BlockSpec((tm, tn), lambda i, j: (i, j))],
        out_specs=pl.BlockSpec((tm, tn), lambda i, j: (i, j)),
        out_shape=jax.ShapeDtypeStruct((n, nout), F32),
        compiler_params=_params("parallel", "parallel"),
        name="matmul_residual",
    )(a, w, res)


def _sigmoid(x):
    return 1.0 / (1.0 + jnp.exp(-x))


def _ref_rows(a_ref, half, block):
    span = 2 * half
    parts = [jnp.broadcast_to(a_ref[g * span + half - 1:g * span + half, :], (span, a_ref.shape[1]))
             for g in range(block // span)]
    return parts[0] if len(parts) == 1 else jnp.concatenate(parts, axis=0)


def _hgrn_kernel(hn_ref, w_ref, lb_ref, og_ref, tri_ref, o_ref, proj_ref, cum_ref, *, seq, block):
    dim = HGRN_DIM
    proj_ref[...] = jnp.dot(hn_ref[0], w_ref[0], preferred_element_type=F32)
    lb = lb_ref[0]
    og = og_ref[...]
    tri = tri_ref[...]
    t_idx = lax.broadcasted_iota(jnp.int32, (block, block), 0)
    s_idx = lax.broadcasted_iota(jnp.int32, (block, block), 1)
    causal = s_idx <= t_idx
    split = t_idx ^ s_idx
    halves = []
    half = block // 2
    while half >= HGRN_SUB:
        halves.append(half)
        half //= 2
    contract_last = (((1,), (1,)), ((), ()))
    contract_first = (((0,), (0,)), ((), ()))

    def body(bi, state_t):
        r0 = pl.multiple_of(bi * block, block)
        pq = proj_ref[pl.ds(r0, block), 0 * dim:1 * dim]
        pf = proj_ref[pl.ds(r0, block), 1 * dim:2 * dim]
        v = proj_ref[pl.ds(r0, block), 2 * dim:3 * dim]
        pg = proj_ref[pl.ds(r0, block), 3 * dim:4 * dim]
        q = pq * _sigmoid(pq)
        forget = lb + (1.0 - lb) * _sigmoid(pf)
        k = 1.0 - forget
        a = jnp.log(forget)
        a1 = a.astype(BF16)
        rem = a - a1.astype(F32)
        a2 = rem.astype(BF16)
        a3 = (rem - a2.astype(F32)).astype(BF16)
        cum = (jnp.dot(tri, a1, preferred_element_type=F32) + jnp.dot(tri, a2, preferred_element_type=F32)
               + jnp.dot(tri, a3, preferred_element_type=F32))
        cum_ref[...] = cum
        total = cum_ref[block - 1:block, :]
        v16 = v.astype(BF16)
        scores = None
        for half in halves:
            e = jnp.exp(-jnp.abs(cum - _ref_rows(cum_ref, half, block)))
            p = lax.dot_general((q * e).astype(BF16), (k * e).astype(BF16), contract_last,
                                preferred_element_type=F32)
            scores = p if scores is None else jnp.where(split >= 2 * half, scores, p)
        sub_ref = jnp.concatenate(
            [jnp.zeros((HGRN_SUB, dim), F32)]
            + [jnp.broadcast_to(cum_ref[j * HGRN_SUB - 1:j * HGRN_SUB, :], (HGRN_SUB, dim))
               for j in range(1, block // HGRN_SUB)], axis=0)
        local = cum - sub_ref
        p = lax.dot_general((q * jnp.exp(local)).astype(BF16), (k * jnp.exp(-local)).astype(BF16),
                            contract_last, preferred_element_type=F32)
        scores = p if scores is None else jnp.where(split >= HGRN_SUB, scores, p)
        scores = jnp.where(causal, scores, 0.0)
        o = jnp.dot(scores.astype(BF16), v16, preferred_element_type=F32)
        o = o + lax.dot_general((q * jnp.exp(cum)).astype(BF16), state_t.astype(BF16), contract_last,
                                preferred_element_type=F32)
        k_end = (k * jnp.exp(total - cum)).astype(BF16)
        state_t = state_t * jnp.exp(total) + lax.dot_general(v16, k_end, contract_first,
                                                              preferred_element_type=F32)
        ms = jnp.mean(o * o, axis=-1, keepdims=True)
        y = o * lax.rsqrt(ms + NORM_EPS) * og
        o_ref[0, pl.ds(r0, block), :] = (y * (pg * _sigmoid(pg))).astype(o_ref.dtype)
        return state_t

    lax.fori_loop(0, seq // block, body, jnp.zeros((dim, dim), F32))


def _hgrn_mixer(hn, w_heads, lb, out_gain):
    b, s, d = hn.shape
    dim = HGRN_DIM
    block = min(HGRN_BLOCK, s)
    tri = jnp.asarray(np.tril(np.ones((block, block), np.float32)), BF16)
    return pl.pallas_call(
        functools.partial(_hgrn_kernel, seq=s, block=block),
        grid=(b, HGRN_HEADS),
        in_specs=[pl.BlockSpec((1, s, d), lambda i, h: (i, 0, 0)),
                  pl.BlockSpec((1, d, 4 * dim), lambda i, h: (h, 0, 0)),
                  pl.BlockSpec((1, 1, dim), lambda i, h: (h, 0, 0)),
                  pl.BlockSpec((1, dim), lambda i, h: (0, 0)),
                  pl.BlockSpec((block, block), lambda i, h: (0, 0))],
        out_specs=pl.BlockSpec((1, s, dim), lambda i, h: (i, 0, h)),
        out_shape=jax.ShapeDtypeStruct((b, s, d), BF16),
        scratch_shapes=[pltpu.VMEM((s, 4 * dim), F32), pltpu.VMEM((block, dim), F32)],
        compiler_params=_params("parallel", "arbitrary"),
        name="hgrn_mixer",
    )(hn, w_heads, lb.reshape(HGRN_HEADS, 1, dim), out_gain.reshape(1, dim).astype(F32), tri)


def _qkv_kernel(a_ref, w_ref, g_ref, cos_ref, sin_ref, o_ref, *, n_rot_blocks, tn):
    acc = jnp.dot(a_ref[...], w_ref[...], preferred_element_type=F32)
    j = pl.program_id(1)

    @pl.when(j < n_rot_blocks)
    def _():
        cosf = cos_ref[...]
        sins = sin_ref[...]
        for hh in range(tn // DIFF_HEAD_DIM):
            sl = slice(hh * DIFF_HEAD_DIM, (hh + 1) * DIFF_HEAD_DIM)
            x = acc[:, sl]
            ms = jnp.mean(x * x, axis=-1, keepdims=True)
            y = x * lax.rsqrt(ms + NORM_EPS) * g_ref[:, sl]
            rot = pltpu.roll(y, DIFF_HEAD_DIM // 2, 1)
            o_ref[:, sl] = (y * cosf + rot * sins).astype(o_ref.dtype)

    @pl.when(j >= n_rot_blocks)
    def _():
        o_ref[...] = acc.astype(o_ref.dtype)


def _qkv_proj(hn, w, gains, cosf, sins, tm=512, tn=1024):
    n, d = hn.shape
    nout = w.shape[1]
    tm = min(tm, n)
    n_rot_blocks = (4 * DIFF_HEADS * DIFF_HEAD_DIM) // tn
    return pl.pallas_call(
        functools.partial(_qkv_kernel, n_rot_blocks=n_rot_blocks, tn=tn),
        grid=(n // tm, nout // tn),
        in_specs=[pl.BlockSpec((tm, d), lambda i, j: (i, 0)),
                  pl.BlockSpec((d, tn), lambda i, j: (0, j)),
                  pl.BlockSpec((1, tn), lambda i, j: (0, j)),
                  pl.BlockSpec((tm, DIFF_HEAD_DIM), lambda i, j: (i, 0)),
                  pl.BlockSpec((tm, DIFF_HEAD_DIM), lambda i, j: (i, 0))],
        out_specs=pl.BlockSpec((tm, tn), lambda i, j: (i, j)),
        out_shape=jax.ShapeDtypeStruct((n, nout), BF16),
        compiler_params=_params("parallel", "arbitrary"),
        name="qkv_proj",
    )(hn, w, gains, cosf, sins)


def _attn_kernel(q_ref, k_ref, v_ref, lam_ref, sg_ref, o_ref, m_ref, l_ref, acc_ref, *, seq, tile, lam_init):
    hd = DIFF_HEAD_DIM
    lf = lam_ref[...]
    lam = (jnp.exp(jnp.sum(lf[0:1] * lf[1:2], axis=-1, keepdims=True))
           - jnp.exp(jnp.sum(lf[2:3] * lf[3:4], axis=-1, keepdims=True)) + lam_init)
    row = lax.broadcasted_iota(jnp.int32, (tile, tile), 0)
    col = lax.broadcasted_iota(jnp.int32, (tile, tile), 1)
    contract_last = (((1,), (1,)), ((), ()))

    def qblock(i, carry):
        q0 = pl.multiple_of(i * tile, tile)
        for c in range(2):
            q = q_ref[0, pl.ds(q0, tile), c * hd:(c + 1) * hd]
            kd = k_ref[0, pl.ds(q0, tile), c * hd:(c + 1) * hd]
            s = lax.dot_general(q, kd, contract_last, preferred_element_type=F32)
            s = jnp.where(col <= row, s, -jnp.inf)
            m = jnp.max(s, axis=-1, keepdims=True)
            e = jnp.exp(s - m)
            m_ref[c] = jnp.broadcast_to(m, (tile, LANES))
            l_ref[c] = jnp.broadcast_to(jnp.sum(e, axis=-1, keepdims=True), (tile, LANES))
            acc_ref[c] = jnp.dot(e.astype(BF16), v_ref[0, pl.ds(q0, tile), :], preferred_element_type=F32)

        def kvblock(jb, carry2):
            k0 = pl.multiple_of(jb * tile, tile)
            vj = v_ref[0, pl.ds(k0, tile), :]
            for c in range(2):
                q = q_ref[0, pl.ds(q0, tile), c * hd:(c + 1) * hd]
                kj = k_ref[0, pl.ds(k0, tile), c * hd:(c + 1) * hd]
                s = lax.dot_general(q, kj, contract_last, preferred_element_type=F32)
                m_old = m_ref[c][:, 0:1]
                m_new = jnp.maximum(m_old, jnp.max(s, axis=-1, keepdims=True))
                alpha = jnp.exp(m_old - m_new)
                e = jnp.exp(s - m_new)
                l_ref[c] = jnp.broadcast_to(alpha * l_ref[c][:, 0:1] + jnp.sum(e, axis=-1, keepdims=True),
                                            (tile, LANES))
                m_ref[c] = jnp.broadcast_to(m_new, (tile, LANES))
                acc_ref[c] = alpha * acc_ref[c] + jnp.dot(e.astype(BF16), vj, preferred_element_type=F32)
            return carry2

        lax.fori_loop(0, i, kvblock, 0)
        o = acc_ref[0] / l_ref[0][:, 0:1] - lam * (acc_ref[1] / l_ref[1][:, 0:1])
        ms = jnp.mean(o * o, axis=-1, keepdims=True)
        y = o * lax.rsqrt(ms + NORM_EPS) * sg_ref[...]
        o_ref[0, pl.ds(q0, tile), :] = (y * (1.0 - lam_init)).astype(o_ref.dtype)
        return carry

    lax.fori_loop(0, seq // tile, qblock, 0)


def _diff_attention(qkv, lam, sub_gain, lam_init):
    b, s, _ = qkv.shape
    hd = DIFF_HEAD_DIM
    tile = min(ATTN_TILE, s)
    nh = DIFF_HEADS
    return pl.pallas_call(
        functools.partial(_attn_kernel, seq=s, tile=tile, lam_init=lam_init),
        grid=(b, nh),
        in_specs=[pl.BlockSpec((1, s, 2 * hd), lambda i, h: (i, 0, h)),
                  pl.BlockSpec((1, s, 2 * hd), lambda i, h: (i, 0, nh + h)),
                  pl.BlockSpec((1, s, 2 * hd), lambda i, h: (i, 0, 2 * nh + h)),
                  pl.BlockSpec((4, hd), lambda i, h: (0, 0)),
                  pl.BlockSpec((1, 2 * hd), lambda i, h: (0, 0))],
        out_specs=pl.BlockSpec((1, s, 2 * hd), lambda i, h: (i, 0, h)),
        out_shape=jax.ShapeDtypeStruct((b, s, nh * 2 * hd), BF16),
        scratch_shapes=[pltpu.VMEM((2, tile, LANES), F32), pltpu.VMEM((2, tile, LANES), F32),
                        pltpu.VMEM((2, tile, 2 * hd), F32)],
        compiler_params=_params("parallel", "parallel"),
        name="diff_attention",
    )(qkv, qkv, qkv, lam.astype(F32), sub_gain.reshape(1, 2 * hd).astype(F32))


def _router_kernel(h_ref, g_ref, w_ref, xp_ref, meta_ref):
    x = h_ref[...]
    ms = jnp.mean(x * x, axis=-1, keepdims=True)
    xn = x * lax.rsqrt(ms + NORM_EPS) * g_ref[...]
    x_hi = xn.astype(BF16)
    x_hi32 = x_hi.astype(F32)
    x_lo = (xn - x_hi32).astype(BF16)
    half = xn.shape[1] // 2
    bits = pltpu.bitcast(x_hi32, jnp.uint32)
    xp_ref[...] = (bits[:, :half] >> 16) | (bits[:, half:] & jnp.uint32(0xFFFF0000))
    both = jnp.dot(x_hi, w_ref[...], preferred_element_type=F32)
    lg = both[:, :LANES] + both[:, LANES:] + jnp.dot(x_lo, w_ref[:, :LANES], preferred_element_type=F32)
    lane = lax.broadcasted_iota(jnp.int32, lg.shape, 1)
    is_group = lane < N_GROUPS
    gmax = jnp.max(jnp.where(is_group, lg, -jnp.inf), axis=-1, keepdims=True)
    ge = jnp.where(is_group, jnp.exp(lg - gmax), 0.0)
    gp = ge / jnp.sum(ge, axis=-1, keepdims=True)
    g_top_p = jnp.max(jnp.where(is_group, gp, -1.0), axis=-1, keepdims=True)
    g_top = jnp.min(jnp.where(is_group & (gp == g_top_p), lane, LANES), axis=-1, keepdims=True)
    first = N_GROUPS + g_top * EXPERTS_PER_GROUP
    in_group = (lane >= first) & (lane < first + EXPERTS_PER_GROUP)
    emax = jnp.max(jnp.where(in_group, lg, -jnp.inf), axis=-1, keepdims=True)
    ee = jnp.where(in_group, jnp.exp(lg - emax), 0.0)
    ep = ee / jnp.sum(ee, axis=-1, keepdims=True)
    p1 = jnp.max(jnp.where(in_group, ep, -1.0), axis=-1, keepdims=True)
    i1 = jnp.min(jnp.where(in_group & (ep == p1), lane, LANES), axis=-1, keepdims=True)
    rest = in_group & (lane != i1)
    p2 = jnp.max(jnp.where(rest, ep, -1.0), axis=-1, keepdims=True)
    i2 = jnp.min(jnp.where(rest & (ep == p2), lane, LANES), axis=-1, keepdims=True)
    denom = p1 + p2
    gate1 = g_top_p * p1 / denom
    gate2 = g_top_p * p2 / denom
    e1 = (i1 - N_GROUPS).astype(F32)
    e2 = (i2 - N_GROUPS).astype(F32)
    meta_ref[...] = jnp.where(lane == 0, e1, jnp.where(lane == 1, e2, jnp.where(lane == 2, gate1,
                              jnp.where(lane == 3, gate2, 0.0))))


def _router(h, gain, w_split, tm=256):
    n, d = h.shape
    tm = min(tm, n)
    return pl.pallas_call(
        _router_kernel,
        grid=(n // tm,),
        in_specs=[pl.BlockSpec((tm, d), lambda i: (i, 0)),
                  pl.BlockSpec((1, d), lambda i: (0, 0)),
                  pl.BlockSpec((d, 2 * LANES), lambda i: (0, 0))],
        out_specs=[pl.BlockSpec((tm, d // 2), lambda i: (i, 0)), pl.BlockSpec((tm, LANES), lambda i: (i, 0))],
        out_shape=[jax.ShapeDtypeStruct((n, d // 2), jnp.uint32), jax.ShapeDtypeStruct((n, LANES), F32)],
        compiler_params=_params("parallel"),
        name="moe_router",
    )(h, gain.reshape(1, d).astype(F32), w_split)


def _row_copy(src_hbm, dst_ref, src_row, dst_row, sem):
    return pltpu.make_async_copy(src_hbm.at[pl.ds(src_row, 1)], dst_ref.at[pl.ds(dst_row, 1)], sem)


def _gather_kernel(idx_ref, src_hbm, dst_hbm, sem, *, rows):
    base = pl.program_id(0) * rows

    def start(r, c):
        _row_copy(src_hbm, dst_hbm, idx_ref[base + r], base + r, sem).start()
        return c

    def wait(r, c):
        _row_copy(src_hbm, dst_hbm, 0, base + r, sem).wait()
        return c

    lax.fori_loop(0, rows, start, 0)
    lax.fori_loop(0, rows, wait, 0)


def _gather_rows(src, row_idx, rows=EXPERT_ROWS):
    n_rows = row_idx.shape[0]
    return pl.pallas_call(
        functools.partial(_gather_kernel, rows=rows),
        grid_spec=pltpu.PrefetchScalarGridSpec(
            num_scalar_prefetch=1,
            grid=(n_rows // rows,),
            in_specs=[pl.BlockSpec(memory_space=pl.ANY)],
            out_specs=pl.BlockSpec(memory_space=pl.ANY),
            scratch_shapes=[pltpu.SemaphoreType.DMA(())]),
        out_shape=jax.ShapeDtypeStruct((n_rows, src.shape[1]), src.dtype),
        compiler_params=_params("arbitrary"),
        name="moe_gather",
    )(row_idx, src)


def _expert_kernel(be_ref, nb_ref, x_ref, wg_ref, wu_ref, wd_ref, y_ref, wg16, wu16, wd16):
    i = pl.program_id(0)
    new_expert = jnp.logical_or(i == 0, be_ref[i] != be_ref[jnp.maximum(i - 1, 0)])

    @pl.when(jnp.logical_and(i < nb_ref[0], new_expert))
    def _():
        wg16[...] = wg_ref[...].astype(BF16)
        wu16[...] = wu_ref[...].astype(BF16)
        wd16[...] = wd_ref[...].astype(BF16)

    @pl.when(i < nb_ref[0])
    def _():
        w = x_ref[...]
        lo = pltpu.bitcast(w << 16, F32)
        hi = pltpu.bitcast(w & jnp.uint32(0xFFFF0000), F32)
        x = jnp.concatenate([lo, hi], axis=1).astype(BF16)
        g = jnp.dot(x, wg16[...], preferred_element_type=F32)
        u = jnp.dot(x, wu16[...], preferred_element_type=F32)
        hmid = (g * _sigmoid(g)) * u
        y_ref[...] = jnp.dot(hmid.astype(BF16), wd16[...], preferred_element_type=F32)

    @pl.when(i >= nb_ref[0])
    def _():
        y_ref[...] = jnp.zeros_like(y_ref)


def _expert_mlp(x_sorted, block_expert, n_used, w_gate, w_up, w_down, rows=EXPERT_ROWS):
    n_rows, half = x_sorted.shape
    d = 2 * half
    ff = w_gate.shape[-1]
    return pl.pallas_call(
        _expert_kernel,
        grid_spec=pltpu.PrefetchScalarGridSpec(
            num_scalar_prefetch=2,
            grid=(n_rows // rows,),
            in_specs=[pl.BlockSpec((rows, half), lambda i, be, nb: (i, 0)),
                      pl.BlockSpec((None, d, ff), lambda i, be, nb: (be[i], 0, 0)),
                      pl.BlockSpec((None, d, ff), lambda i, be, nb: (be[i], 0, 0)),
                      pl.BlockSpec((None, ff, d), lambda i, be, nb: (be[i], 0, 0))],
            out_specs=pl.BlockSpec((rows, d), lambda i, be, nb: (i, 0)),
            scratch_shapes=[pltpu.VMEM((d, ff), BF16), pltpu.VMEM((d, ff), BF16), pltpu.VMEM((ff, d), BF16)]),
        out_shape=jax.ShapeDtypeStruct((n_rows, d), F32),
        compiler_params=_params("arbitrary"),
        name="moe_experts",
    )(block_expert, n_used, x_sorted, w_gate, w_up, w_down)


def _combine_kernel(s0_ref, s1_ref, h_ref, meta_ref, y_hbm, o_ref, buf, sem, *, rows):
    base = pl.program_id(0) * rows

    def start(r, c):
        _row_copy(y_hbm, buf.at[0], s0_ref[base + r], r, sem).start()
        _row_copy(y_hbm, buf.at[1], s1_ref[base + r], r, sem).start()
        return c

    def wait(r, c):
        _row_copy(y_hbm, buf.at[0], 0, r, sem).wait()
        _row_copy(y_hbm, buf.at[1], 0, r, sem).wait()
        return c

    lax.fori_loop(0, rows, start, 0)
    lax.fori_loop(0, rows, wait, 0)
    meta = meta_ref[...]
    o_ref[...] = h_ref[...] + (buf[0] * meta[:, 2:3] + buf[1] * meta[:, 3:4])


def _combine(h, meta, y_sorted, slot0, slot1, rows=256):
    n, d = h.shape
    rows = min(rows, n)
    return pl.pallas_call(
        functools.partial(_combine_kernel, rows=rows),
        grid_spec=pltpu.PrefetchScalarGridSpec(
            num_scalar_prefetch=2,
            grid=(n // rows,),
            in_specs=[pl.BlockSpec((rows, d), lambda i, a, b: (i, 0)),
                      pl.BlockSpec((rows, LANES), lambda i, a, b: (i, 0)),
                      pl.BlockSpec(memory_space=pl.ANY)],
            out_specs=pl.BlockSpec((rows, d), lambda i, a, b: (i, 0)),
            scratch_shapes=[pltpu.VMEM((2, rows, d), F32), pltpu.SemaphoreType.DMA(())]),
        out_shape=jax.ShapeDtypeStruct((n, d), F32),
        compiler_params=_params("arbitrary"),
        name="moe_combine",
    )(slot0, slot1, h, meta, y_sorted)


def _split_router_weights(w_group, w_expert):
    d = w_group.shape[0]
    w = jnp.concatenate([w_group.astype(F32), w_expert.astype(F32),
                         jnp.zeros((d, LANES - N_GROUPS - N_EXPERTS), F32)], axis=1)
    hi = w.astype(BF16)
    lo = (w - hi.astype(F32)).astype(BF16)
    return jnp.concatenate([hi, lo], axis=1)


def _hier_moe(h, ffn_gain, w_group, w_expert, w_gate, w_up, w_down):
    n, d = h.shape
    rows = EXPERT_ROWS
    xp, meta = _router(h, ffn_gain, _split_router_weights(w_group, w_expert))
    expert = meta[:, :TOP_K].astype(jnp.int32).reshape(-1)
    n_assign = n * TOP_K
    n_blocks = n_assign // rows + N_EXPERTS
    n_rows = n_blocks * rows
    onehot = (expert[:, None] == jnp.arange(N_EXPERTS, dtype=jnp.int32)[None, :]).astype(jnp.int32)
    running = jnp.cumsum(onehot, axis=0)
    counts = running[-1]
    rank = jnp.sum((running - onehot) * onehot, axis=1)
    pad_counts = (counts + rows - 1) // rows * rows
    pad_end = jnp.cumsum(pad_counts)
    pad_start = pad_end - pad_counts
    slot = pad_start[expert] + rank
    tok = jnp.arange(n_assign, dtype=jnp.int32) // TOP_K
    row_tok = jnp.zeros((n_rows,), jnp.int32).at[slot].set(tok)
    n_used = (pad_end[-1] // rows).astype(jnp.int32).reshape(1)
    blk = jnp.minimum(jnp.arange(n_blocks, dtype=jnp.int32), n_used[0] - 1)
    block_expert = jnp.minimum(jnp.searchsorted(pad_end, blk * rows, side='right'), N_EXPERTS - 1).astype(jnp.int32)
    slots = slot.reshape(n, TOP_K)
    x_sorted = _gather_rows(xp, row_tok, rows)
    y_sorted = _expert_mlp(x_sorted, block_expert, n_used, w_gate, w_up, w_down, rows)
    return _combine(h, meta, y_sorted, slots[:, 0], slots[:, 1])


def kernel(x, positions, mix_norm, ffn_norm, hgrn_w_in, hgrn_out_gain, hgrn_w_out, hgrn_lb_logits, diff_w_in, diff_q_gain, diff_k_gain, diff_lambda, diff_sub_gain, diff_w_out, moe_w_group, moe_w_expert, moe_w_gate, moe_w_up, moe_w_down):
    b, s, d = x.shape
    n = b * s
    depth = mix_norm.shape[0]
    lower_bounds = jnp.cumsum(jax.nn.softmax(hgrn_lb_logits.astype(F32), axis=0), axis=0)
    h = x.reshape(n, d)
    for layer in range(depth):
        j = layer // N_MIXERS
        hn = _rmsnorm_bf16(h, mix_norm[layer])
        if layer % N_MIXERS == 0:
            w_heads = (hgrn_w_in[j].reshape(d, 4, HGRN_HEADS, HGRN_DIM).transpose(2, 0, 1, 3)
                       .reshape(HGRN_HEADS, d, 4 * HGRN_DIM).astype(BF16))
            o = _hgrn_mixer(hn.reshape(b, s, d), w_heads, lower_bounds[layer], hgrn_out_gain[j])
            h = _matmul_residual(o.reshape(n, d), hgrn_w_out[j].astype(BF16), h)
        else:
            lam_init = 0.8 - 0.6 * math.exp(-0.3 * layer)
            hd = DIFF_HEAD_DIM
            half = hd // 2
            inv_freq = ROPE_THETA ** (-jnp.arange(half, dtype=F32) / half)
            ang = positions.astype(F32).reshape(n, 1) * inv_freq[None, :]
            cosf = jnp.concatenate([jnp.cos(ang), jnp.cos(ang)], axis=1)
            sins = jnp.concatenate([-jnp.sin(ang), jnp.sin(ang)], axis=1)
            n_qk = 2 * DIFF_HEADS
            gains = jnp.concatenate([jnp.tile(diff_q_gain[j].astype(F32), n_qk) * (hd ** -0.5),
                                     jnp.tile(diff_k_gain[j].astype(F32), n_qk),
                                     jnp.ones((2 * DIFF_HEADS * hd,), F32)]).reshape(1, -1)
            qkv = _qkv_proj(hn, diff_w_in[j].astype(BF16), gains, cosf, sins)
            o = _diff_attention(qkv.reshape(b, s, -1), diff_lambda[j], diff_sub_gain[j], lam_init)
            h = _matmul_residual(o.reshape(n, d), diff_w_out[j].astype(BF16), h)
        h = _hier_moe(h, ffn_norm[layer], moe_w_group[layer], moe_w_expert[layer],
                      moe_w_gate[layer], moe_w_up[layer], moe_w_down[layer])
    return h.reshape(b, s, d)
```

```python
import functools
import math

import jax
import jax.numpy as jnp
import numpy as np
from jax import lax
from jax.experimental import pallas as pl
from jax.experimental.pallas import tpu as pltpu

F32 = jnp.float32
BF16 = jnp.bfloat16

NORM_EPS = 1e-6
N_MIXERS = 2
HGRN_HEADS = 16
HGRN_DIM = 128
HGRN_BLOCK = 128
HGRN_SUB = 16
DIFF_HEADS = 8
DIFF_HEAD_DIM = 128
ROPE_THETA = 10000.0
ATTN_TILE = 256
N_GROUPS = 8
EXPERTS_PER_GROUP = 8
N_EXPERTS = N_GROUPS * EXPERTS_PER_GROUP
TOP_K = 2
EXPERT_ROWS = 256
LANES = 128
VMEM_LIMIT_BYTES = 56 * 1024 * 1024


def _params(*semantics):
    return pltpu.CompilerParams(dimension_semantics=semantics, vmem_limit_bytes=VMEM_LIMIT_BYTES)


def _norm_kernel(h_ref, g_ref, o_ref):
    x = h_ref[...]
    ms = jnp.mean(x * x, axis=-1, keepdims=True)
    o_ref[...] = (x * lax.rsqrt(ms + NORM_EPS) * g_ref[...]).astype(o_ref.dtype)


def _rmsnorm_bf16(h, gain, tm=512):
    n, d = h.shape
    return pl.pallas_call(
        _norm_kernel,
        grid=(n // tm,),
        in_specs=[pl.BlockSpec((tm, d), lambda i: (i, 0)), pl.BlockSpec((1, d), lambda i: (0, 0))],
        out_specs=pl.BlockSpec((tm, d), lambda i: (i, 0)),
        out_shape=jax.ShapeDtypeStruct((n, d), BF16),
        compiler_params=_params("parallel"),
        name="rmsnorm_bf16",
    )(h, gain.reshape(1, d).astype(F32))


def _mm_res_kernel(a_ref, w_ref, r_ref, o_ref):
    o_ref[...] = r_ref[...] + jnp.dot(a_ref[...], w_ref[...], preferred_element_type=F32)


def _matmul_residual(a, w, res, tm=1024, tn=1024):
    n, k = a.shape
    nout = w.shape[1]
    tm, tn = min(tm, n), min(tn, nout)
    return pl.pallas_call(
        _mm_res_kernel,
        grid=(n // tm, nout // tn),
        in_specs=[pl.BlockSpec((tm, k), lambda i, j: (i, 0)),
                  pl.BlockSpec((k, tn), lambda i, j: (0, j)),
                  pl.BlockSpec((tm, tn), lambda i, j: (i, j))],
        out_specs=pl.BlockSpec((tm, tn), lambda i, j: (i, j)),
        out_shape=jax.ShapeDtypeStruct((n, nout), F32),
        compiler_params=_params("parallel", "parallel"),
        name="matmul_residual",
    )(a, w, res)


def _sigmoid(x):
    return 1.0 / (1.0 + jnp.exp(-x))


def _ref_rows(a_ref, half, block):
    span = 2 * half
    parts = [jnp.broadcast_to(a_ref[g * span + half - 1:g * span + half, :], (span, a_ref.shape[1]))
             for g in range(block // span)]
    return parts[0] if len(parts) == 1 else jnp.concatenate(parts, axis=0)


def _hgrn_kernel(hn_ref, w_ref, lb_ref, og_ref, tri_ref, o_ref, proj_ref, cum_ref, *, seq, block):
    dim = HGRN_DIM
    proj_ref[...] = jnp.dot(hn_ref[0], w_ref[0], preferred_element_type=F32)
    lb = lb_ref[0]
    og = og_ref[...]
    tri = tri_ref[...]
    t_idx = lax.broadcasted_iota(jnp.int32, (block, block), 0)
    s_idx = lax.broadcasted_iota(jnp.int32, (block, block), 1)
    causal = s_idx <= t_idx
    split = t_idx ^ s_idx
    halves = []
    half = block // 2
    while half >= HGRN_SUB:
        halves.append(half)
        half //= 2
    contract_last = (((1,), (1,)), ((), ()))
    contract_first = (((0,), (0,)), ((), ()))

    def body(bi, state_t):
        r0 = pl.multiple_of(bi * block, block)
        pq = proj_ref[pl.ds(r0, block), 0 * dim:1 * dim]
        pf = proj_ref[pl.ds(r0, block), 1 * dim:2 * dim]
        v = proj_ref[pl.ds(r0, block), 2 * dim:3 * dim]
        pg = proj_ref[pl.ds(r0, block), 3 * dim:4 * dim]
        q = pq * _sigmoid(pq)
        forget = lb + (1.0 - lb) * _sigmoid(pf)
        k = 1.0 - forget
        a = jnp.log(forget)
        a1 = a.astype(BF16)
        rem = a - a1.astype(F32)
        a2 = rem.astype(BF16)
        a3 = (rem - a2.astype(F32)).astype(BF16)
        cum = (jnp.dot(tri, a1, preferred_element_type=F32) + jnp.dot(tri, a2, preferred_element_type=F32)
               + jnp.dot(tri, a3, preferred_element_type=F32))
        cum_ref[...] = cum
        total = cum_ref[block - 1:block, :]
        v16 = v.astype(BF16)
        scores = None
        for half in halves:
            e = jnp.exp(-jnp.abs(cum - _ref_rows(cum_ref, half, block)))
            p = lax.dot_general((q * e).astype(BF16), (k * e).astype(BF16), contract_last,
                                preferred_element_type=F32)
            scores = p if scores is None else jnp.where(split >= 2 * half, scores, p)
        sub_ref = jnp.concatenate(
            [jnp.zeros((HGRN_SUB, dim), F32)]
            + [jnp.broadcast_to(cum_ref[j * HGRN_SUB - 1:j * HGRN_SUB, :], (HGRN_SUB, dim))
               for j in range(1, block // HGRN_SUB)], axis=0)
        local = cum - sub_ref
        p = lax.dot_general((q * jnp.exp(local)).astype(BF16), (k * jnp.exp(-local)).astype(BF16),
                            contract_last, preferred_element_type=F32)
        scores = p if scores is None else jnp.where(split >= HGRN_SUB, scores, p)
        scores = jnp.where(causal, scores, 0.0)
        o = jnp.dot(scores.astype(BF16), v16, preferred_element_type=F32)
        o = o + lax.dot_general((q * jnp.exp(cum)).astype(BF16), state_t.astype(BF16), contract_last,
                                preferred_element_type=F32)
        k_end = (k * jnp.exp(total - cum)).astype(BF16)
        state_t = state_t * jnp.exp(total) + lax.dot_general(v16, k_end, contract_first,
                                                              preferred_element_type=F32)
        ms = jnp.mean(o * o, axis=-1, keepdims=True)
        y = o * lax.rsqrt(ms + NORM_EPS) * og
        o_ref[0, pl.ds(r0, block), :] = (y * (pg * _sigmoid(pg))).astype(o_ref.dtype)
        return state_t

    lax.fori_loop(0, seq // block, body, jnp.zeros((dim, dim), F32))


def _hgrn_mixer(hn, w_heads, lb, out_gain):
    b, s, d = hn.shape
    dim = HGRN_DIM
    block = min(HGRN_BLOCK, s)
    tri = jnp.asarray(np.tril(np.ones((block, block), np.float32)), BF16)
    return pl.pallas_call(
        functools.partial(_hgrn_kernel, seq=s, block=block),
        grid=(b, HGRN_HEADS),
        in_specs=[pl.BlockSpec((1, s, d), lambda i, h: (i, 0, 0)),
                  pl.BlockSpec((1, d, 4 * dim), lambda i, h: (h, 0, 0)),
                  pl.BlockSpec((1, 1, dim), lambda i, h: (h, 0, 0)),
                  pl.BlockSpec((1, dim), lambda i, h: (0, 0)),
                  pl.BlockSpec((block, block), lambda i, h: (0, 0))],
        out_specs=pl.BlockSpec((1, s, dim), lambda i, h: (i, 0, h)),
        out_shape=jax.ShapeDtypeStruct((b, s, d), BF16),
        scratch_shapes=[pltpu.VMEM((s, 4 * dim), F32), pltpu.VMEM((block, dim), F32)],
        compiler_params=_params("parallel", "arbitrary"),
        name="hgrn_mixer",
    )(hn, w_heads, lb.reshape(HGRN_HEADS, 1, dim), out_gain.reshape(1, dim).astype(F32), tri)


def _qkv_kernel(a_ref, w_ref, g_ref, cos_ref, sin_ref, o_ref, *, n_rot_blocks, tn):
    acc = jnp.dot(a_ref[...], w_ref[...], preferred_element_type=F32)
    j = pl.program_id(1)

    @pl.when(j < n_rot_blocks)
    def _():
        cosf = cos_ref[...]
        sins = sin_ref[...]
        for hh in range(tn // DIFF_HEAD_DIM):
            sl = slice(hh * DIFF_HEAD_DIM, (hh + 1) * DIFF_HEAD_DIM)
            x = acc[:, sl]
            ms = jnp.mean(x * x, axis=-1, keepdims=True)
            y = x * lax.rsqrt(ms + NORM_EPS) * g_ref[:, sl]
            rot = pltpu.roll(y, DIFF_HEAD_DIM // 2, 1)
            o_ref[:, sl] = (y * cosf + rot * sins).astype(o_ref.dtype)

    @pl.when(j >= n_rot_blocks)
    def _():
        o_ref[...] = acc.astype(o_ref.dtype)


def _qkv_proj(hn, w, gains, cosf, sins, tm=512, tn=1024):
    n, d = hn.shape
    nout = w.shape[1]
    tm = min(tm, n)
    n_rot_blocks = (4 * DIFF_HEADS * DIFF_HEAD_DIM) // tn
    return pl.pallas_call(
        functools.partial(_qkv_kernel, n_rot_blocks=n_rot_blocks, tn=tn),
        grid=(n // tm, nout // tn),
        in_specs=[pl.BlockSpec((tm, d), lambda i, j: (i, 0)),
                  pl.BlockSpec((d, tn), lambda i, j: (0, j)),
                  pl.BlockSpec((1, tn), lambda i, j: (0, j)),
                  pl.BlockSpec((tm, DIFF_HEAD_DIM), lambda i, j: (i, 0)),
                  pl.BlockSpec((tm, DIFF_HEAD_DIM), lambda i, j: (i, 0))],
        out_specs=pl.BlockSpec((tm, tn), lambda i, j: (i, j)),
        out_shape=jax.ShapeDtypeStruct((n, nout), BF16),
        compiler_params=_params("parallel", "arbitrary"),
        name="qkv_proj",
    )(hn, w, gains, cosf, sins)


def _attn_kernel(q_ref, k_ref, v_ref, lam_ref, sg_ref, o_ref, s_ref, m_ref, l_ref, acc_ref, *, seq, tile, lam_init):
    hd = DIFF_HEAD_DIM
    lf = lam_ref[...]
    lam = (jnp.exp(jnp.sum(lf[0:1] * lf[1:2], axis=-1, keepdims=True))
           - jnp.exp(jnp.sum(lf[2:3] * lf[3:4], axis=-1, keepdims=True)) + lam_init)
    row = lax.broadcasted_iota(jnp.int32, (tile, tile), 0)
    col = lax.broadcasted_iota(jnp.int32, (tile, tile), 1)
    contract_last = (((1,), (1,)), ((), ()))

    def lane_fold(x, op):
        out = x[:, :LANES]
        for t in range(1, tile // LANES):
            out = op(out, x[:, t * LANES:(t + 1) * LANES])
        return out

    def qblock(i, carry):
        q0 = pl.multiple_of(i * tile, tile)

        def score_tile(jb, c2):
            k0 = pl.multiple_of(jb * tile, tile)
            for c in range(2):
                q = q_ref[0, pl.ds(q0, tile), c * hd:(c + 1) * hd]
                kj = k_ref[0, pl.ds(k0, tile), c * hd:(c + 1) * hd]
                s = lax.dot_general(q, kj, contract_last, preferred_element_type=F32)
                s_ref[c, :, pl.ds(k0, tile)] = s
                m_ref[c] = jnp.maximum(m_ref[c], lane_fold(s, jnp.maximum))
            return c2

        for c in range(2):
            m_ref[c] = jnp.full((tile, LANES), -jnp.inf, F32)
        lax.fori_loop(0, i, score_tile, 0)
        for c in range(2):
            q = q_ref[0, pl.ds(q0, tile), c * hd:(c + 1) * hd]
            kd = k_ref[0, pl.ds(q0, tile), c * hd:(c + 1) * hd]
            s = lax.dot_general(q, kd, contract_last, preferred_element_type=F32)
            s = jnp.where(col <= row, s, -jnp.inf)
            s_ref[c, :, pl.ds(q0, tile)] = s
            m_part = jnp.maximum(m_ref[c], lane_fold(s, jnp.maximum))
            m_ref[c] = jnp.broadcast_to(jnp.max(m_part, axis=-1, keepdims=True), (tile, LANES))
            l_ref[c] = jnp.zeros((tile, LANES), F32)
            acc_ref[c] = jnp.zeros((tile, 2 * hd), F32)

        def value_tile(jb, c2):
            k0 = pl.multiple_of(jb * tile, tile)
            vj = v_ref[0, pl.ds(k0, tile), :]
            for c in range(2):
                m = m_ref[c]
                e = jnp.exp(s_ref[c, :, pl.ds(k0, tile)] - jnp.concatenate([m] * (tile // LANES), axis=1))
                l_ref[c] = l_ref[c] + lane_fold(e, jnp.add)
                acc_ref[c] = acc_ref[c] + jnp.dot(e.astype(BF16), vj, preferred_element_type=F32)
            return c2

        lax.fori_loop(0, i + 1, value_tile, 0)
        inv = [1.0 / jnp.sum(l_ref[c], axis=-1, keepdims=True) for c in range(2)]
        o = acc_ref[0] * inv[0] - lam * (acc_ref[1] * inv[1])
        ms = jnp.mean(o * o, axis=-1, keepdims=True)
        y = o * lax.rsqrt(ms + NORM_EPS) * sg_ref[...]
        o_ref[0, pl.ds(q0, tile), :] = (y * (1.0 - lam_init)).astype(o_ref.dtype)
        return carry

    lax.fori_loop(0, seq // tile, qblock, 0)


def _diff_attention(qkv, lam, sub_gain, lam_init):
    b, s, _ = qkv.shape
    hd = DIFF_HEAD_DIM
    tile = min(ATTN_TILE, s)
    nh = DIFF_HEADS
    return pl.pallas_call(
        functools.partial(_attn_kernel, seq=s, tile=tile, lam_init=lam_init),
        grid=(b, nh),
        in_specs=[pl.BlockSpec((1, s, 2 * hd), lambda i, h: (i, 0, h)),
                  pl.BlockSpec((1, s, 2 * hd), lambda i, h: (i, 0, nh + h)),
                  pl.BlockSpec((1, s, 2 * hd), lambda i, h: (i, 0, 2 * nh + h)),
                  pl.BlockSpec((4, hd), lambda i, h: (0, 0)),
                  pl.BlockSpec((1, 2 * hd), lambda i, h: (0, 0))],
        out_specs=pl.BlockSpec((1, s, 2 * hd), lambda i, h: (i, 0, h)),
        out_shape=jax.ShapeDtypeStruct((b, s, nh * 2 * hd), BF16),
        scratch_shapes=[pltpu.VMEM((2, tile, s), F32), pltpu.VMEM((2, tile, LANES), F32),
                        pltpu.VMEM((2, tile, LANES), F32), pltpu.VMEM((2, tile, 2 * hd), F32)],
        compiler_params=_params("parallel", "parallel"),
        name="diff_attention",
    )(qkv, qkv, qkv, lam.astype(F32), sub_gain.reshape(1, 2 * hd).astype(F32))


def _router_kernel(h_ref, g_ref, w_ref, xn_ref, meta_ref):
    x = h_ref[...]
    ms = jnp.mean(x * x, axis=-1, keepdims=True)
    xn = x * lax.rsqrt(ms + NORM_EPS) * g_ref[...]
    xn_ref[...] = xn
    x_hi = xn.astype(BF16)
    x_lo = (xn - x_hi.astype(F32)).astype(BF16)
    both = jnp.dot(x_hi, w_ref[...], preferred_element_type=F32)
    lg = both[:, :LANES] + both[:, LANES:] + jnp.dot(x_lo, w_ref[:, :LANES], preferred_element_type=F32)
    lane = lax.broadcasted_iota(jnp.int32, lg.shape, 1)
    is_group = lane < N_GROUPS
    gmax = jnp.max(jnp.where(is_group, lg, -jnp.inf), axis=-1, keepdims=True)
    ge = jnp.where(is_group, jnp.exp(lg - gmax), 0.0)
    gp = ge / jnp.sum(ge, axis=-1, keepdims=True)
    g_top_p = jnp.max(jnp.where(is_group, gp, -1.0), axis=-1, keepdims=True)
    g_top = jnp.min(jnp.where(is_group & (gp == g_top_p), lane, LANES), axis=-1, keepdims=True)
    first = N_GROUPS + g_top * EXPERTS_PER_GROUP
    in_group = (lane >= first) & (lane < first + EXPERTS_PER_GROUP)
    emax = jnp.max(jnp.where(in_group, lg, -jnp.inf), axis=-1, keepdims=True)
    ee = jnp.where(in_group, jnp.exp(lg - emax), 0.0)
    ep = ee / jnp.sum(ee, axis=-1, keepdims=True)
    p1 = jnp.max(jnp.where(in_group, ep, -1.0), axis=-1, keepdims=True)
    i1 = jnp.min(jnp.where(in_group & (ep == p1), lane, LANES), axis=-1, keepdims=True)
    rest = in_group & (lane != i1)
    p2 = jnp.max(jnp.where(rest, ep, -1.0), axis=-1, keepdims=True)
    i2 = jnp.min(jnp.where(rest & (ep == p2), lane, LANES), axis=-1, keepdims=True)
    denom = p1 + p2
    gate1 = g_top_p * p1 / denom
    gate2 = g_top_p * p2 / denom
    e1 = (i1 - N_GROUPS).astype(F32)
    e2 = (i2 - N_GROUPS).astype(F32)
    meta_ref[...] = jnp.where(lane == 0, e1, jnp.where(lane == 1, e2, jnp.where(lane == 2, gate1,
                              jnp.where(lane == 3, gate2, 0.0))))


def _router(h, gain, w_split, tm=256):
    n, d = h.shape
    tm = min(tm, n)
    return pl.pallas_call(
        _router_kernel,
        grid=(n // tm,),
        in_specs=[pl.BlockSpec((tm, d), lambda i: (i, 0)),
                  pl.BlockSpec((1, d), lambda i: (0, 0)),
                  pl.BlockSpec((d, 2 * LANES), lambda i: (0, 0))],
        out_specs=[pl.BlockSpec((tm, d), lambda i: (i, 0)), pl.BlockSpec((tm, LANES), lambda i: (i, 0))],
        out_shape=[jax.ShapeDtypeStruct((n, d), F32), jax.ShapeDtypeStruct((n, LANES), F32)],
        compiler_params=_params("parallel"),
        name="moe_router",
    )(h, gain.reshape(1, d).astype(F32), w_split)


def _row_copy(src_hbm, dst_ref, src_row, dst_row, sem):
    return pltpu.make_async_copy(src_hbm.at[pl.ds(src_row, 1)], dst_ref.at[pl.ds(dst_row, 1)], sem)


def _expert_kernel(be_ref, nb_ref, tok_ref, xp_hbm, wg_ref, wu_ref, wd_ref, y_ref, xbuf, sems, wg16, wu16, wd16,
                   *, rows):
    i = pl.program_id(0)
    n_used = nb_ref[0]
    slot = lax.rem(i, 2)

    def gather(block, into):
        base = block * rows

        def start(r, c):
            _row_copy(xp_hbm, xbuf.at[into], tok_ref[base + r], r, sems.at[into]).start()
            return c

        lax.fori_loop(0, rows, start, 0, unroll=8)

    @pl.when(i == 0)
    def _():
        gather(0, 0)

    @pl.when(i + 1 < n_used)
    def _():
        gather(i + 1, 1 - slot)

    new_expert = jnp.logical_or(i == 0, be_ref[i] != be_ref[jnp.maximum(i - 1, 0)])

    @pl.when(jnp.logical_and(i < n_used, new_expert))
    def _():
        wg16[...] = wg_ref[...].astype(BF16)
        wu16[...] = wu_ref[...].astype(BF16)
        wd16[...] = wd_ref[...].astype(BF16)

    @pl.when(i < n_used)
    def _():
        pltpu.make_async_copy(xp_hbm.at[pl.ds(0, rows)], xbuf.at[slot], sems.at[slot]).wait()
        x = xbuf[slot].astype(BF16)
        g = jnp.dot(x, wg16[...], preferred_element_type=F32)
        u = jnp.dot(x, wu16[...], preferred_element_type=F32)
        hmid = (g * _sigmoid(g)) * u
        y_ref[...] = jnp.dot(hmid.astype(BF16), wd16[...], preferred_element_type=F32)

    @pl.when(i >= n_used)
    def _():
        y_ref[...] = jnp.zeros_like(y_ref)


def _expert_mlp(xp, row_tok, block_expert, n_used, w_gate, w_up, w_down, rows=EXPERT_ROWS):
    d = xp.shape[1]
    n_rows = row_tok.shape[0]
    ff = w_gate.shape[-1]
    return pl.pallas_call(
        functools.partial(_expert_kernel, rows=rows),
        grid_spec=pltpu.PrefetchScalarGridSpec(
            num_scalar_prefetch=3,
            grid=(n_rows // rows,),
            in_specs=[pl.BlockSpec(memory_space=pl.ANY),
                      pl.BlockSpec((None, d, ff), lambda i, be, nb, tok: (be[i], 0, 0)),
                      pl.BlockSpec((None, d, ff), lambda i, be, nb, tok: (be[i], 0, 0)),
                      pl.BlockSpec((None, ff, d), lambda i, be, nb, tok: (be[i], 0, 0))],
            out_specs=pl.BlockSpec((rows, d), lambda i, be, nb, tok: (i, 0)),
            scratch_shapes=[pltpu.VMEM((2, rows, d), F32), pltpu.SemaphoreType.DMA((2,)),
                            pltpu.VMEM((d, ff), BF16), pltpu.VMEM((d, ff), BF16), pltpu.VMEM((ff, d), BF16)]),
        out_shape=jax.ShapeDtypeStruct((n_rows, d), F32),
        compiler_params=_params("arbitrary"),
        name="moe_experts",
    )(block_expert, n_used, row_tok, xp, w_gate, w_up, w_down)


def _combine_kernel(s0_ref, s1_ref, h_ref, meta_ref, y_hbm, o_ref, buf, sem, *, rows):
    base = pl.program_id(0) * rows

    def start(r, c):
        _row_copy(y_hbm, buf.at[0], s0_ref[base + r], r, sem).start()
        _row_copy(y_hbm, buf.at[1], s1_ref[base + r], r, sem).start()
        return c

    lax.fori_loop(0, rows, start, 0, unroll=8)
    for half in range(2):
        pltpu.make_async_copy(y_hbm.at[pl.ds(0, rows)], buf.at[half], sem).wait()
    meta = meta_ref[...]
    o_ref[...] = h_ref[...] + (buf[0] * meta[:, 2:3] + buf[1] * meta[:, 3:4])


def _combine(h, meta, y_sorted, slot0, slot1, rows=256):
    n, d = h.shape
    rows = min(rows, n)
    return pl.pallas_call(
        functools.partial(_combine_kernel, rows=rows),
        grid_spec=pltpu.PrefetchScalarGridSpec(
            num_scalar_prefetch=2,
            grid=(n // rows,),
            in_specs=[pl.BlockSpec((rows, d), lambda i, a, b: (i, 0)),
                      pl.BlockSpec((rows, LANES), lambda i, a, b: (i, 0)),
                      pl.BlockSpec(memory_space=pl.ANY)],
            out_specs=pl.BlockSpec((rows, d), lambda i, a, b: (i, 0)),
            scratch_shapes=[pltpu.VMEM((2, rows, d), F32), pltpu.SemaphoreType.DMA(())]),
        out_shape=jax.ShapeDtypeStruct((n, d), F32),
        compiler_params=_params("arbitrary"),
        name="moe_combine",
    )(slot0, slot1, h, meta, y_sorted)


def _split_router_weights(w_group, w_expert):
    d = w_group.shape[0]
    w = jnp.concatenate([w_group.astype(F32), w_expert.astype(F32),
                         jnp.zeros((d, LANES - N_GROUPS - N_EXPERTS), F32)], axis=1)
    hi = w.astype(BF16)
    lo = (w - hi.astype(F32)).astype(BF16)
    return jnp.concatenate([hi, lo], axis=1)


def _hier_moe(h, ffn_gain, w_group, w_expert, w_gate, w_up, w_down, layer):
    n, d = h.shape
    rows = EXPERT_ROWS
    xp, meta = _router(h, ffn_gain, _split_router_weights(w_group, w_expert))
    expert = meta[:, :TOP_K].astype(jnp.int32).reshape(-1)
    n_assign = n * TOP_K
    n_blocks = n_assign // rows + N_EXPERTS
    n_rows = n_blocks * rows
    onehot = (expert[:, None] == jnp.arange(N_EXPERTS, dtype=jnp.int32)[None, :]).astype(jnp.int32)
    running = jnp.cumsum(onehot, axis=0)
    counts = running[-1]
    rank = jnp.sum((running - onehot) * onehot, axis=1)
    pad_counts = (counts + rows - 1) // rows * rows
    pad_end = jnp.cumsum(pad_counts)
    pad_start = pad_end - pad_counts
    slot = pad_start[expert] + rank
    tok = jnp.arange(n_assign, dtype=jnp.int32) // TOP_K
    row_tok = jnp.zeros((n_rows,), jnp.int32).at[slot].set(tok)
    n_used = (pad_end[-1] // rows).astype(jnp.int32).reshape(1)
    blk = jnp.minimum(jnp.arange(n_blocks, dtype=jnp.int32), n_used[0] - 1)
    block_expert = jnp.sum((pad_end[None, :] <= (blk * rows)[:, None]).astype(jnp.int32), axis=1)
    block_expert = jnp.minimum(block_expert, N_EXPERTS - 1) + layer * N_EXPERTS
    slots = slot.reshape(n, TOP_K)
    y_sorted = _expert_mlp(xp, row_tok, block_expert, n_used, w_gate, w_up, w_down, rows)
    return _combine(h, meta, y_sorted, slots[:, 0], slots[:, 1])


def kernel(x, positions, mix_norm, ffn_norm, hgrn_w_in, hgrn_out_gain, hgrn_w_out, hgrn_lb_logits, diff_w_in, diff_q_gain, diff_k_gain, diff_lambda, diff_sub_gain, diff_w_out, moe_w_group, moe_w_expert, moe_w_gate, moe_w_up, moe_w_down):
    b, s, d = x.shape
    n = b * s
    depth = mix_norm.shape[0]
    lower_bounds = jnp.cumsum(jax.nn.softmax(hgrn_lb_logits.astype(F32), axis=0), axis=0)
    h = x.reshape(n, d)
    w_gate = moe_w_gate.reshape((-1,) + moe_w_gate.shape[2:])
    w_up = moe_w_up.reshape((-1,) + moe_w_up.shape[2:])
    w_down = moe_w_down.reshape((-1,) + moe_w_down.shape[2:])
    for layer in range(depth):
        j = layer // N_MIXERS
        hn = _rmsnorm_bf16(h, mix_norm[layer])
        if layer % N_MIXERS == 0:
            w_heads = (hgrn_w_in[j].reshape(d, 4, HGRN_HEADS, HGRN_DIM).transpose(2, 0, 1, 3)
                       .reshape(HGRN_HEADS, d, 4 * HGRN_DIM).astype(BF16))
            o = _hgrn_mixer(hn.reshape(b, s, d), w_heads, lower_bounds[layer], hgrn_out_gain[j])
            h = _matmul_residual(o.reshape(n, d), hgrn_w_out[j].astype(BF16), h)
        else:
            lam_init = 0.8 - 0.6 * math.exp(-0.3 * layer)
            hd = DIFF_HEAD_DIM
            half = hd // 2
            inv_freq = ROPE_THETA ** (-jnp.arange(half, dtype=F32) / half)
            ang = positions.astype(F32).reshape(n, 1) * inv_freq[None, :]
            cosf = jnp.concatenate([jnp.cos(ang), jnp.cos(ang)], axis=1)
            sins = jnp.concatenate([-jnp.sin(ang), jnp.sin(ang)], axis=1)
            n_qk = 2 * DIFF_HEADS
            gains = jnp.concatenate([jnp.tile(diff_q_gain[j].astype(F32), n_qk) * (hd ** -0.5),
                                     jnp.tile(diff_k_gain[j].astype(F32), n_qk),
                                     jnp.ones((2 * DIFF_HEADS * hd,), F32)]).reshape(1, -1)
            qkv = _qkv_proj(hn, diff_w_in[j].astype(BF16), gains, cosf, sins)
            o = _diff_attention(qkv.reshape(b, s, -1), diff_lambda[j], diff_sub_gain[j], lam_init)
            h = _matmul_residual(o.reshape(n, d), diff_w_out[j].astype(BF16), h)
        h = _hier_moe(h, ffn_norm[layer], moe_w_group[layer], moe_w_expert[layer], w_gate, w_up, w_down, layer)
    return h.reshape(b, s, d)
```

```python
import functools
import math

import jax
import jax.numpy as jnp
import numpy as np
from jax import lax
from jax.experimental import pallas as pl
from jax.experimental.pallas import tpu as pltpu

F32 = jnp.float32
BF16 = jnp.bfloat16

NORM_EPS = 1e-6
N_MIXERS = 2
HGRN_HEADS = 16
HGRN_DIM = 128
HGRN_BLOCK = 128
HGRN_SUB = 16
DIFF_HEADS = 8
DIFF_HEAD_DIM = 128
ROPE_THETA = 10000.0
ATTN_TILE = 256
N_GROUPS = 8
EXPERTS_PER_GROUP = 8
N_EXPERTS = N_GROUPS * EXPERTS_PER_GROUP
TOP_K = 2
EXPERT_ROWS = 256
LANES = 128
VMEM_LIMIT_BYTES = 56 * 1024 * 1024


def _params(*semantics):
    return pltpu.CompilerParams(dimension_semantics=semantics, vmem_limit_bytes=VMEM_LIMIT_BYTES)


def _norm_kernel(h_ref, g_ref, o_ref):
    x = h_ref[...]
    ms = jnp.mean(x * x, axis=-1, keepdims=True)
    o_ref[...] = (x * lax.rsqrt(ms + NORM_EPS) * g_ref[...]).astype(o_ref.dtype)


def _rmsnorm_bf16(h, gain, tm=512):
    n, d = h.shape
    return pl.pallas_call(
        _norm_kernel,
        grid=(n // tm,),
        in_specs=[pl.BlockSpec((tm, d), lambda i: (i, 0)), pl.BlockSpec((1, d), lambda i: (0, 0))],
        out_specs=pl.BlockSpec((tm, d), lambda i: (i, 0)),
        out_shape=jax.ShapeDtypeStruct((n, d), BF16),
        compiler_params=_params("parallel"),
        name="rmsnorm_bf16",
    )(h, gain.reshape(1, d).astype(F32))


def _mm_res_kernel(a_ref, w_ref, r_ref, o_ref):
    o_ref[...] = r_ref[...] + jnp.dot(a_ref[...], w_ref[...], preferred_element_type=F32)


def _matmul_residual(a, w, res, tm=1024, tn=1024):
    n, k = a.shape
    nout = w.shape[1]
    tm, tn = min(tm, n), min(tn, nout)
    return pl.pallas_call(
        _mm_res_kernel,
        grid=(n // tm, nout // tn),
        in_specs=[pl.BlockSpec((tm, k), lambda i, j: (i, 0)),
                  pl.BlockSpec((k, tn), lambda i, j: (0, j)),
                  pl.BlockSpec((tm, tn), lambda i, j: (i, j))],
        out_specs=pl.BlockSpec((tm, tn), lambda i, j: (i, j)),
        out_shape=jax.ShapeDtypeStruct((n, nout), F32),
        compiler_params=_params("parallel", "parallel"),
        name="matmul_residual",
    )(a, w, res)


def _sigmoid(x):
    return 1.0 / (1.0 + jnp.exp(-x))


def _ref_rows(a_ref, half, block):
    span = 2 * half
    parts = [jnp.broadcast_to(a_ref[g * span + half - 1:g * span + half, :], (span, a_ref.shape[1]))
             for g in range(block // span)]
    return parts[0] if len(parts) == 1 else jnp.concatenate(parts, axis=0)


def _cumsum_rows(x, row):
    shift = 1
    while shift < x.shape[0]:
        x = x + jnp.where(row >= shift, pltpu.roll(x, shift, 0), 0.0)
        shift *= 2
    return x


def _hgrn_kernel(hn_ref, w_ref, lb_ref, og_ref, o_ref, proj_ref, cum_ref, intra_ref, q0_ref, upd_ref, dec_ref,
                 st_ref, *, seq, block):
    dim = HGRN_DIM
    n_blocks = seq // block
    proj_ref[...] = jnp.dot(hn_ref[0], w_ref[0], preferred_element_type=F32)
    lb = lb_ref[0]
    og = og_ref[...]
    t_idx = lax.broadcasted_iota(jnp.int32, (block, block), 0)
    s_idx = lax.broadcasted_iota(jnp.int32, (block, block), 1)
    causal = s_idx <= t_idx
    split = t_idx ^ s_idx
    row = lax.broadcasted_iota(jnp.int32, (block, dim), 0)
    halves = []
    half = block // 2
    while half >= HGRN_SUB:
        halves.append(half)
        half //= 2
    contract_last = (((1,), (1,)), ((), ()))
    contract_first = (((0,), (0,)), ((), ()))

    def local(bi, c):
        r0 = pl.multiple_of(bi * block, block)
        rows = pl.ds(r0, block)
        pq = proj_ref[rows, 0 * dim:1 * dim]
        pf = proj_ref[rows, 1 * dim:2 * dim]
        v16 = proj_ref[rows, 2 * dim:3 * dim].astype(BF16)
        q = pq * _sigmoid(pq)
        forget = lb + (1.0 - lb) * _sigmoid(pf)
        k = 1.0 - forget
        cum = _cumsum_rows(jnp.log(forget), row)
        cum_ref[rows, :] = cum
        blk = cum_ref.at[rows]
        total = blk[block - 1:block, :]
        scores = None
        for half in halves:
            e = jnp.exp(-jnp.abs(cum - _ref_rows(blk, half, block)))
            p = lax.dot_general((q * e).astype(BF16), (k * e).astype(BF16), contract_last,
                                preferred_element_type=F32)
            scores = p if scores is None else jnp.where(split >= 2 * half, scores, p)
        sub_ref = jnp.concatenate(
            [jnp.zeros((HGRN_SUB, dim), F32)]
            + [jnp.broadcast_to(blk[j * HGRN_SUB - 1:j * HGRN_SUB, :], (HGRN_SUB, dim))
               for j in range(1, block // HGRN_SUB)], axis=0)
        loc = cum - sub_ref
        p = lax.dot_general((q * jnp.exp(loc)).astype(BF16), (k * jnp.exp(-loc)).astype(BF16),
                            contract_last, preferred_element_type=F32)
        scores = p if scores is None else jnp.where(split >= HGRN_SUB, scores, p)
        scores = jnp.where(causal, scores, 0.0)
        intra_ref[rows, :] = jnp.dot(scores.astype(BF16), v16, preferred_element_type=F32)
        q0_ref[rows, :] = (q * jnp.exp(cum)).astype(BF16)
        k_end = (k * jnp.exp(total - cum)).astype(BF16)
        upd_ref[bi] = lax.dot_general(v16, k_end, contract_first, preferred_element_type=F32)
        dec_ref[bi] = jnp.exp(total)
        return c

    lax.fori_loop(0, n_blocks, local, 0, unroll=2)

    def carry_state(bi, state_t):
        st_ref[bi] = state_t.astype(BF16)
        return state_t * dec_ref[bi] + upd_ref[bi]

    lax.fori_loop(0, n_blocks, carry_state, jnp.zeros((dim, dim), F32))

    def finish(bi, c):
        r0 = pl.multiple_of(bi * block, block)
        rows = pl.ds(r0, block)
        o = intra_ref[rows, :] + lax.dot_general(q0_ref[rows, :], st_ref[bi], contract_last,
                                                 preferred_element_type=F32)
        ms = jnp.mean(o * o, axis=-1, keepdims=True)
        y = o * lax.rsqrt(ms + NORM_EPS) * og
        pg = proj_ref[rows, 3 * dim:4 * dim]
        o_ref[0, rows, :] = (y * (pg * _sigmoid(pg))).astype(o_ref.dtype)
        return c

    lax.fori_loop(0, n_blocks, finish, 0, unroll=2)


def _hgrn_mixer(hn, w_heads, lb, out_gain):
    b, s, d = hn.shape
    dim = HGRN_DIM
    block = min(HGRN_BLOCK, s)
    n_blocks = s // block
    return pl.pallas_call(
        functools.partial(_hgrn_kernel, seq=s, block=block),
        grid=(b, HGRN_HEADS),
        in_specs=[pl.BlockSpec((1, s, d), lambda i, h: (i, 0, 0)),
                  pl.BlockSpec((1, d, 4 * dim), lambda i, h: (h, 0, 0)),
                  pl.BlockSpec((1, 1, dim), lambda i, h: (h, 0, 0)),
                  pl.BlockSpec((1, dim), lambda i, h: (0, 0))],
        out_specs=pl.BlockSpec((1, s, dim), lambda i, h: (i, 0, h)),
        out_shape=jax.ShapeDtypeStruct((b, s, d), BF16),
        scratch_shapes=[pltpu.VMEM((s, 4 * dim), F32),
                        pltpu.VMEM((s, dim), F32),
                        pltpu.VMEM((s, dim), F32),
                        pltpu.VMEM((s, dim), BF16),
                        pltpu.VMEM((n_blocks, dim, dim), F32),
                        pltpu.VMEM((n_blocks, 1, dim), F32),
                        pltpu.VMEM((n_blocks, dim, dim), BF16)],
        compiler_params=_params("parallel", "arbitrary"),
        name="hgrn_mixer",
    )(hn, w_heads, lb.reshape(HGRN_HEADS, 1, dim), out_gain.reshape(1, dim).astype(F32))


def _qkv_kernel(a_ref, w_ref, g_ref, cos_ref, sin_ref, o_ref, *, n_rot_blocks, tn):
    acc = jnp.dot(a_ref[...], w_ref[...], preferred_element_type=F32)
    j = pl.program_id(1)

    @pl.when(j < n_rot_blocks)
    def _():
        cosf = cos_ref[...]
        sins = sin_ref[...]
        for hh in range(tn // DIFF_HEAD_DIM):
            sl = slice(hh * DIFF_HEAD_DIM, (hh + 1) * DIFF_HEAD_DIM)
            x = acc[:, sl]
            ms = jnp.mean(x * x, axis=-1, keepdims=True)
            y = x * lax.rsqrt(ms + NORM_EPS) * g_ref[:, sl]
            rot = pltpu.roll(y, DIFF_HEAD_DIM // 2, 1)
            o_ref[:, sl] = (y * cosf + rot * sins).astype(o_ref.dtype)

    @pl.when(j >= n_rot_blocks)
    def _():
        o_ref[...] = acc.astype(o_ref.dtype)


def _qkv_proj(hn, w, gains, cosf, sins, tm=512, tn=1024):
    n, d = hn.shape
    nout = w.shape[1]
    tm = min(tm, n)
    n_rot_blocks = (4 * DIFF_HEADS * DIFF_HEAD_DIM) // tn
    return pl.pallas_call(
        functools.partial(_qkv_kernel, n_rot_blocks=n_rot_blocks, tn=tn),
        grid=(n // tm, nout // tn),
        in_specs=[pl.BlockSpec((tm, d), lambda i, j: (i, 0)),
                  pl.BlockSpec((d, tn), lambda i, j: (0, j)),
                  pl.BlockSpec((1, tn), lambda i, j: (0, j)),
                  pl.BlockSpec((tm, DIFF_HEAD_DIM), lambda i, j: (i, 0)),
                  pl.BlockSpec((tm, DIFF_HEAD_DIM), lambda i, j: (i, 0))],
        out_specs=pl.BlockSpec((tm, tn), lambda i, j: (i, j)),
        out_shape=jax.ShapeDtypeStruct((n, nout), BF16),
        compiler_params=_params("parallel", "arbitrary"),
        name="qkv_proj",
    )(hn, w, gains, cosf, sins)


def _attn_kernel(q_ref, k_ref, v_ref, lam_ref, sg_ref, o_ref, s_ref, m_ref, l_ref, acc_ref, *, seq, tile, lam_init):
    hd = DIFF_HEAD_DIM
    lf = lam_ref[...]
    lam = (jnp.exp(jnp.sum(lf[0:1] * lf[1:2], axis=-1, keepdims=True))
           - jnp.exp(jnp.sum(lf[2:3] * lf[3:4], axis=-1, keepdims=True)) + lam_init)
    row = lax.broadcasted_iota(jnp.int32, (tile, tile), 0)
    col = lax.broadcasted_iota(jnp.int32, (tile, tile), 1)
    contract_last = (((1,), (1,)), ((), ()))

    def lane_fold(x, op):
        out = x[:, :LANES]
        for t in range(1, tile // LANES):
            out = op(out, x[:, t * LANES:(t + 1) * LANES])
        return out

    def qblock(i, carry):
        q0 = pl.multiple_of(i * tile, tile)

        def score_tile(jb, c2):
            k0 = pl.multiple_of(jb * tile, tile)
            for c in range(2):
                q = q_ref[0, pl.ds(q0, tile), c * hd:(c + 1) * hd]
                kj = k_ref[0, pl.ds(k0, tile), c * hd:(c + 1) * hd]
                s = lax.dot_general(q, kj, contract_last, preferred_element_type=F32)
                s_ref[c, :, pl.ds(k0, tile)] = s
                m_ref[c] = jnp.maximum(m_ref[c], lane_fold(s, jnp.maximum))
            return c2

        for c in range(2):
            m_ref[c] = jnp.full((tile, LANES), -jnp.inf, F32)
        lax.fori_loop(0, i, score_tile, 0)
        for c in range(2):
            q = q_ref[0, pl.ds(q0, tile), c * hd:(c + 1) * hd]
            kd = k_ref[0, pl.ds(q0, tile), c * hd:(c + 1) * hd]
            s = lax.dot_general(q, kd, contract_last, preferred_element_type=F32)
            s = jnp.where(col <= row, s, -jnp.inf)
            s_ref[c, :, pl.ds(q0, tile)] = s
            m_part = jnp.maximum(m_ref[c], lane_fold(s, jnp.maximum))
            m_ref[c] = jnp.broadcast_to(jnp.max(m_part, axis=-1, keepdims=True), (tile, LANES))
            l_ref[c] = jnp.zeros((tile, LANES), F32)
            acc_ref[c] = jnp.zeros((tile, 2 * hd), F32)

        def value_tile(jb, c2):
            k0 = pl.multiple_of(jb * tile, tile)
            vj = v_ref[0, pl.ds(k0, tile), :]
            for c in range(2):
                m = m_ref[c]
                e = jnp.exp(s_ref[c, :, pl.ds(k0, tile)] - jnp.concatenate([m] * (tile // LANES), axis=1))
                l_ref[c] = l_ref[c] + lane_fold(e, jnp.add)
                acc_ref[c] = acc_ref[c] + jnp.dot(e.astype(BF16), vj, preferred_element_type=F32)
            return c2

        lax.fori_loop(0, i + 1, value_tile, 0)
        inv = [1.0 / jnp.sum(l_ref[c], axis=-1, keepdims=True) for c in range(2)]
        o = acc_ref[0] * inv[0] - lam * (acc_ref[1] * inv[1])
        ms = jnp.mean(o * o, axis=-1, keepdims=True)
        y = o * lax.rsqrt(ms + NORM_EPS) * sg_ref[...]
        o_ref[0, pl.ds(q0, tile), :] = (y * (1.0 - lam_init)).astype(o_ref.dtype)
        return carry

    lax.fori_loop(0, seq // tile, qblock, 0)


def _diff_attention(qkv, lam, sub_gain, lam_init):
    b, s, _ = qkv.shape
    hd = DIFF_HEAD_DIM
    tile = min(ATTN_TILE, s)
    nh = DIFF_HEADS
    return pl.pallas_call(
        functools.partial(_attn_kernel, seq=s, tile=tile, lam_init=lam_init),
        grid=(b, nh),
        in_specs=[pl.BlockSpec((1, s, 2 * hd), lambda i, h: (i, 0, h)),
                  pl.BlockSpec((1, s, 2 * hd), lambda i, h: (i, 0, nh + h)),
                  pl.BlockSpec((1, s, 2 * hd), lambda i, h: (i, 0, 2 * nh + h)),
                  pl.BlockSpec((4, hd), lambda i, h: (0, 0)),
                  pl.BlockSpec((1, 2 * hd), lambda i, h: (0, 0))],
        out_specs=pl.BlockSpec((1, s, 2 * hd), lambda i, h: (i, 0, h)),
        out_shape=jax.ShapeDtypeStruct((b, s, nh * 2 * hd), BF16),
        scratch_shapes=[pltpu.VMEM((2, tile, s), F32), pltpu.VMEM((2, tile, LANES), F32),
                        pltpu.VMEM((2, tile, LANES), F32), pltpu.VMEM((2, tile, 2 * hd), F32)],
        compiler_params=_params("parallel", "parallel"),
        name="diff_attention",
    )(qkv, qkv, qkv, lam.astype(F32), sub_gain.reshape(1, 2 * hd).astype(F32))


def _router_kernel(h_ref, g_ref, w_ref, xn_ref, meta_ref):
    x = h_ref[...]
    ms = jnp.mean(x * x, axis=-1, keepdims=True)
    xn = x * lax.rsqrt(ms + NORM_EPS) * g_ref[...]
    xn_ref[...] = xn
    x_hi = xn.astype(BF16)
    x_lo = (xn - x_hi.astype(F32)).astype(BF16)
    both = jnp.dot(x_hi, w_ref[...], preferred_element_type=F32)
    lg = both[:, :LANES] + both[:, LANES:] + jnp.dot(x_lo, w_ref[:, :LANES], preferred_element_type=F32)
    lane = lax.broadcasted_iota(jnp.int32, lg.shape, 1)
    is_group = lane < N_GROUPS
    gmax = jnp.max(jnp.where(is_group, lg, -jnp.inf), axis=-1, keepdims=True)
    ge = jnp.where(is_group, jnp.exp(lg - gmax), 0.0)
    gp = ge / jnp.sum(ge, axis=-1, keepdims=True)
    g_top_p = jnp.max(jnp.where(is_group, gp, -1.0), axis=-1, keepdims=True)
    g_top = jnp.min(jnp.where(is_group & (gp == g_top_p), lane, LANES), axis=-1, keepdims=True)
    first = N_GROUPS + g_top * EXPERTS_PER_GROUP
    in_group = (lane >= first) & (lane < first + EXPERTS_PER_GROUP)
    emax = jnp.max(jnp.where(in_group, lg, -jnp.inf), axis=-1, keepdims=True)
    ee = jnp.where(in_group, jnp.exp(lg - emax), 0.0)
    ep = ee / jnp.sum(ee, axis=-1, keepdims=True)
    p1 = jnp.max(jnp.where(in_group, ep, -1.0), axis=-1, keepdims=True)
    i1 = jnp.min(jnp.where(in_group & (ep == p1), lane, LANES), axis=-1, keepdims=True)
    rest = in_group & (lane != i1)
    p2 = jnp.max(jnp.where(rest, ep, -1.0), axis=-1, keepdims=True)
    i2 = jnp.min(jnp.where(rest & (ep == p2), lane, LANES), axis=-1, keepdims=True)
    denom = p1 + p2
    gate1 = g_top_p * p1 / denom
    gate2 = g_top_p * p2 / denom
    e1 = (i1 - N_GROUPS).astype(F32)
    e2 = (i2 - N_GROUPS).astype(F32)
    meta_ref[...] = jnp.where(lane == 0, e1, jnp.where(lane == 1, e2, jnp.where(lane == 2, gate1,
                              jnp.where(lane == 3, gate2, 0.0))))


def _router(h, gain, w_split, tm=256):
    n, d = h.shape
    tm = min(tm, n)
    return pl.pallas_call(
        _router_kernel,
        grid=(n // tm,),
        in_specs=[pl.BlockSpec((tm, d), lambda i: (i, 0)),
                  pl.BlockSpec((1, d), lambda i: (0, 0)),
                  pl.BlockSpec((d, 2 * LANES), lambda i: (0, 0))],
        out_specs=[pl.BlockSpec((tm, d), lambda i: (i, 0)), pl.BlockSpec((tm, LANES), lambda i: (i, 0))],
        out_shape=[jax.ShapeDtypeStruct((n, d), F32), jax.ShapeDtypeStruct((n, LANES), F32)],
        compiler_params=_params("parallel"),
        name="moe_router",
    )(h, gain.reshape(1, d).astype(F32), w_split)


def _row_copy(src_hbm, dst_ref, src_row, dst_row, sem):
    return pltpu.make_async_copy(src_hbm.at[pl.ds(src_row, 1)], dst_ref.at[pl.ds(dst_row, 1)], sem)


def _expert_kernel(be_ref, nb_ref, tok_ref, xp_hbm, wg_ref, wu_ref, wd_ref, y_ref, xbuf, sems, wg16, wu16, wd16,
                   *, rows):
    i = pl.program_id(0)
    n_used = nb_ref[0]
    slot = lax.rem(i, 2)

    def gather(block, into):
        base = block * rows

        def start(r, c):
            _row_copy(xp_hbm, xbuf.at[into], tok_ref[base + r], r, sems.at[into]).start()
            return c

        lax.fori_loop(0, rows, start, 0, unroll=8)

    @pl.when(i == 0)
    def _():
        gather(0, 0)

    @pl.when(i + 1 < n_used)
    def _():
        gather(i + 1, 1 - slot)

    new_expert = jnp.logical_or(i == 0, be_ref[i] != be_ref[jnp.maximum(i - 1, 0)])

    @pl.when(jnp.logical_and(i < n_used, new_expert))
    def _():
        wg16[...] = wg_ref[...].astype(BF16)
        wu16[...] = wu_ref[...].astype(BF16)
        wd16[...] = wd_ref[...].astype(BF16)

    @pl.when(i < n_used)
    def _():
        pltpu.make_async_copy(xp_hbm.at[pl.ds(0, rows)], xbuf.at[slot], sems.at[slot]).wait()
        x = xbuf[slot].astype(BF16)
        g = jnp.dot(x, wg16[...], preferred_element_type=F32)
        u = jnp.dot(x, wu16[...], preferred_element_type=F32)
        hmid = (g * _sigmoid(g)) * u
        y_ref[...] = jnp.dot(hmid.astype(BF16), wd16[...], preferred_element_type=F32)

    @pl.when(i >= n_used)
    def _():
        y_ref[...] = jnp.zeros_like(y_ref)


def _expert_mlp(xp, row_tok, block_expert, n_used, w_gate, w_up, w_down, rows=EXPERT_ROWS):
    d = xp.shape[1]
    n_rows = row_tok.shape[0]
    ff = w_gate.shape[-1]
    return pl.pallas_call(
        functools.partial(_expert_kernel, rows=rows),
        grid_spec=pltpu.PrefetchScalarGridSpec(
            num_scalar_prefetch=3,
            grid=(n_rows // rows,),
            in_specs=[pl.BlockSpec(memory_space=pl.ANY),
                      pl.BlockSpec((None, d, ff), lambda i, be, nb, tok: (be[i], 0, 0)),
                      pl.BlockSpec((None, d, ff), lambda i, be, nb, tok: (be[i], 0, 0)),
                      pl.BlockSpec((None, ff, d), lambda i, be, nb, tok: (be[i], 0, 0))],
            out_specs=pl.BlockSpec((rows, d), lambda i, be, nb, tok: (i, 0)),
            scratch_shapes=[pltpu.VMEM((2, rows, d), F32), pltpu.SemaphoreType.DMA((2,)),
                            pltpu.VMEM((d, ff), BF16), pltpu.VMEM((d, ff), BF16), pltpu.VMEM((ff, d), BF16)]),
        out_shape=jax.ShapeDtypeStruct((n_rows, d), F32),
        compiler_params=_params("arbitrary"),
        name="moe_experts",
    )(block_expert, n_used, row_tok, xp, w_gate, w_up, w_down)


def _combine_kernel(s0_ref, s1_ref, h_ref, meta_ref, y_hbm, o_ref, buf, sem, *, rows):
    base = pl.program_id(0) * rows

    def start(r, c):
        _row_copy(y_hbm, buf.at[0], s0_ref[base + r], r, sem).start()
        _row_copy(y_hbm, buf.at[1], s1_ref[base + r], r, sem).start()
        return c

    lax.fori_loop(0, rows, start, 0, unroll=8)
    for half in range(2):
        pltpu.make_async_copy(y_hbm.at[pl.ds(0, rows)], buf.at[half], sem).wait()
    meta = meta_ref[...]
    o_ref[...] = h_ref[...] + (buf[0] * meta[:, 2:3] + buf[1] * meta[:, 3:4])


def _combine(h, meta, y_sorted, slot0, slot1, rows=256):
    n, d = h.shape
    rows = min(rows, n)
    return pl.pallas_call(
        functools.partial(_combine_kernel, rows=rows),
        grid_spec=pltpu.PrefetchScalarGridSpec(
            num_scalar_prefetch=2,
            grid=(n // rows,),
            in_specs=[pl.BlockSpec((rows, d), lambda i, a, b: (i, 0)),
                      pl.BlockSpec((rows, LANES), lambda i, a, b: (i, 0)),
                      pl.BlockSpec(memory_space=pl.ANY)],
            out_specs=pl.BlockSpec((rows, d), lambda i, a, b: (i, 0)),
            scratch_shapes=[pltpu.VMEM((2, rows, d), F32), pltpu.SemaphoreType.DMA(())]),
        out_shape=jax.ShapeDtypeStruct((n, d), F32),
        compiler_params=_params("arbitrary"),
        name="moe_combine",
    )(slot0, slot1, h, meta, y_sorted)


def _split_router_weights(w_group, w_expert):
    d = w_group.shape[0]
    w = jnp.concatenate([w_group.astype(F32), w_expert.astype(F32),
                         jnp.zeros((d, LANES - N_GROUPS - N_EXPERTS), F32)], axis=1)
    hi = w.astype(BF16)
    lo = (w - hi.astype(F32)).astype(BF16)
    return jnp.concatenate([hi, lo], axis=1)


def _hier_moe(h, ffn_gain, w_group, w_expert, w_gate, w_up, w_down, layer):
    n, d = h.shape
    rows = EXPERT_ROWS
    xp, meta = _router(h, ffn_gain, _split_router_weights(w_group, w_expert))
    expert = meta[:, :TOP_K].astype(jnp.int32).reshape(-1)
    n_assign = n * TOP_K
    n_blocks = n_assign // rows + N_EXPERTS
    n_rows = n_blocks * rows
    onehot = (expert[:, None] == jnp.arange(N_EXPERTS, dtype=jnp.int32)[None, :]).astype(jnp.int32)
    running = jnp.cumsum(onehot, axis=0)
    counts = running[-1]
    rank = jnp.sum((running - onehot) * onehot, axis=1)
    pad_counts = (counts + rows - 1) // rows * rows
    pad_end = jnp.cumsum(pad_counts)
    pad_start = pad_end - pad_counts
    slot = pad_start[expert] + rank
    tok = jnp.arange(n_assign, dtype=jnp.int32) // TOP_K
    row_tok = jnp.zeros((n_rows,), jnp.int32).at[slot].set(tok)
    n_used = (pad_end[-1] // rows).astype(jnp.int32).reshape(1)
    blk = jnp.minimum(jnp.arange(n_blocks, dtype=jnp.int32), n_used[0] - 1)
    block_expert = jnp.sum((pad_end[None, :] <= (blk * rows)[:, None]).astype(jnp.int32), axis=1)
    block_expert = jnp.minimum(block_expert, N_EXPERTS - 1) + layer * N_EXPERTS
    slots = slot.reshape(n, TOP_K)
    y_sorted = _expert_mlp(xp, row_tok, block_expert, n_used, w_gate, w_up, w_down, rows)
    return _combine(h, meta, y_sorted, slots[:, 0], slots[:, 1])


def kernel(x, positions, mix_norm, ffn_norm, hgrn_w_in, hgrn_out_gain, hgrn_w_out, hgrn_lb_logits, diff_w_in, diff_q_gain, diff_k_gain, diff_lambda, diff_sub_gain, diff_w_out, moe_w_group, moe_w_expert, moe_w_gate, moe_w_up, moe_w_down):
    b, s, d = x.shape
    n = b * s
    depth = mix_norm.shape[0]
    lower_bounds = jnp.cumsum(jax.nn.softmax(hgrn_lb_logits.astype(F32), axis=0), axis=0)
    h = x.reshape(n, d)
    w_gate = moe_w_gate.reshape((-1,) + moe_w_gate.shape[2:])
    w_up = moe_w_up.reshape((-1,) + moe_w_up.shape[2:])
    w_down = moe_w_down.reshape((-1,) + moe_w_down.shape[2:])
    for layer in range(depth):
        j = layer // N_MIXERS
        hn = _rmsnorm_bf16(h, mix_norm[layer])
        if layer % N_MIXERS == 0:
            w_heads = (hgrn_w_in[j].reshape(d, 4, HGRN_HEADS, HGRN_DIM).transpose(2, 0, 1, 3)
                       .reshape(HGRN_HEADS, d, 4 * HGRN_DIM).astype(BF16))
            o = _hgrn_mixer(hn.reshape(b, s, d), w_heads, lower_bounds[layer], hgrn_out_gain[j])
            h = _matmul_residual(o.reshape(n, d), hgrn_w_out[j].astype(BF16), h)
        else:
            lam_init = 0.8 - 0.6 * math.exp(-0.3 * layer)
            hd = DIFF_HEAD_DIM
            half = hd // 2
            inv_freq = ROPE_THETA ** (-jnp.arange(half, dtype=F32) / half)
            ang = positions.astype(F32).reshape(n, 1) * inv_freq[None, :]
            cosf = jnp.concatenate([jnp.cos(ang), jnp.cos(ang)], axis=1)
            sins = jnp.concatenate([-jnp.sin(ang), jnp.sin(ang)], axis=1)
            n_qk = 2 * DIFF_HEADS
            gains = jnp.concatenate([jnp.tile(diff_q_gain[j].astype(F32), n_qk) * (hd ** -0.5),
                                     jnp.tile(diff_k_gain[j].astype(F32), n_qk),
                                     jnp.ones((2 * DIFF_HEADS * hd,), F32)]).reshape(1, -1)
            qkv = _qkv_proj(hn, diff_w_in[j].astype(BF16), gains, cosf, sins)
            o = _diff_attention(qkv.reshape(b, s, -1), diff_lambda[j], diff_sub_gain[j], lam_init)
            h = _matmul_residual(o.reshape(n, d), diff_w_out[j].astype(BF16), h)
        h = _hier_moe(h, ffn_norm[layer], moe_w_group[layer], moe_w_expert[layer], w_gate, w_up, w_down, layer)
    return h.reshape(b, s, d)
```

```python
import functools
import math

import jax
import jax.numpy as jnp
import numpy as np
from jax import lax
from jax.experimental import pallas as pl
from jax.experimental.pallas import tpu as pltpu

F32 = jnp.float32
BF16 = jnp.bfloat16

NORM_EPS = 1e-6
N_MIXERS = 2
HGRN_HEADS = 16
HGRN_DIM = 128
HGRN_BLOCK = 128
HGRN_SUB = 16
DIFF_HEADS = 8
DIFF_HEAD_DIM = 128
ROPE_THETA = 10000.0
ATTN_TILE = 256
N_GROUPS = 8
EXPERTS_PER_GROUP = 8
N_EXPERTS = N_GROUPS * EXPERTS_PER_GROUP
TOP_K = 2
EXPERT_ROWS = 256
LANES = 128
VMEM_LIMIT_BYTES = 56 * 1024 * 1024


def _params(*semantics):
    return pltpu.CompilerParams(dimension_semantics=semantics, vmem_limit_bytes=VMEM_LIMIT_BYTES)


def _norm_kernel(h_ref, g_ref, o_ref):
    x = h_ref[...]
    ms = jnp.mean(x * x, axis=-1, keepdims=True)
    o_ref[...] = (x * lax.rsqrt(ms + NORM_EPS) * g_ref[...]).astype(o_ref.dtype)


def _rmsnorm_bf16(h, gain, tm=512):
    n, d = h.shape
    return pl.pallas_call(
        _norm_kernel,
        grid=(n // tm,),
        in_specs=[pl.BlockSpec((tm, d), lambda i: (i, 0)), pl.BlockSpec((1, d), lambda i: (0, 0))],
        out_specs=pl.BlockSpec((tm, d), lambda i: (i, 0)),
        out_shape=jax.ShapeDtypeStruct((n, d), BF16),
        compiler_params=_params("parallel"),
        name="rmsnorm_bf16",
    )(h, gain.reshape(1, d).astype(F32))


def _mm_res_kernel(a_ref, w_ref, r_ref, o_ref):
    o_ref[...] = r_ref[...] + jnp.dot(a_ref[...], w_ref[...], preferred_element_type=F32)


def _matmul_residual(a, w, res, tm=1024, tn=1024):
    n, k = a.shape
    nout = w.shape[1]
    tm, tn = min(tm, n), min(tn, nout)
    return pl.pallas_call(
        _mm_res_kernel,
        grid=(n // tm, nout // tn),
        in_specs=[pl.BlockSpec((tm, k), lambda i, j: (i, 0)),
                  pl.BlockSpec((k, tn), lambda i, j: (0, j)),
                  pl.BlockSpec((tm, tn), lambda i, j: (i, j))],
        out_specs=pl.BlockSpec((tm, tn), lambda i, j: (i, j)),
        out_shape=jax.ShapeDtypeStruct((n, nout), F32),
        compiler_params=_params("parallel", "parallel"),
        name="matmul_residual",
    )(a, w, res)


def _sigmoid(x):
    return 1.0 / (1.0 + jnp.exp(-x))


def _ref_rows(a_ref, half, block):
    span = 2 * half
    parts = [jnp.broadcast_to(a_ref[g * span + half - 1:g * span + half, :], (span, a_ref.shape[1]))
             for g in range(block // span)]
    return parts[0] if len(parts) == 1 else jnp.concatenate(parts, axis=0)


def _cumsum_rows(x, row):
    shift = 1
    while shift < x.shape[0]:
        x = x + jnp.where(row >= shift, pltpu.roll(x, shift, 0), 0.0)
        shift *= 2
    return x


def _hgrn_kernel(hn_ref, w_ref, lb_ref, og_ref, o_ref, proj_ref, cum_ref, intra_ref, q0_ref, upd_ref, dec_ref,
                 st_ref, *, seq, block):
    dim = HGRN_DIM
    n_blocks = seq // block
    proj_ref[...] = jnp.dot(hn_ref[0], w_ref[0], preferred_element_type=F32)
    lb = lb_ref[0]
    og = og_ref[...]
    t_idx = lax.broadcasted_iota(jnp.int32, (block, block), 0)
    s_idx = lax.broadcasted_iota(jnp.int32, (block, block), 1)
    causal = s_idx <= t_idx
    split = t_idx ^ s_idx
    row = lax.broadcasted_iota(jnp.int32, (block, dim), 0)
    halves = []
    half = block // 2
    while half >= HGRN_SUB:
        halves.append(half)
        half //= 2
    contract_last = (((1,), (1,)), ((), ()))
    contract_first = (((0,), (0,)), ((), ()))

    def local(bi, c):
        r0 = pl.multiple_of(bi * block, block)
        rows = pl.ds(r0, block)
        pq = proj_ref[rows, 0 * dim:1 * dim]
        pf = proj_ref[rows, 1 * dim:2 * dim]
        v16 = proj_ref[rows, 2 * dim:3 * dim].astype(BF16)
        q = pq * _sigmoid(pq)
        forget = lb + (1.0 - lb) * _sigmoid(pf)
        k = 1.0 - forget
        cum = _cumsum_rows(jnp.log(forget), row)
        cum_ref[rows, :] = cum
        blk = cum_ref.at[rows]
        total = blk[block - 1:block, :]
        scores = None
        for half in halves:
            e = jnp.exp(-jnp.abs(cum - _ref_rows(blk, half, block)))
            p = lax.dot_general((q * e).astype(BF16), (k * e).astype(BF16), contract_last,
                                preferred_element_type=F32)
            scores = p if scores is None else jnp.where(split >= 2 * half, scores, p)
        sub_ref = jnp.concatenate(
            [jnp.zeros((HGRN_SUB, dim), F32)]
            + [jnp.broadcast_to(blk[j * HGRN_SUB - 1:j * HGRN_SUB, :], (HGRN_SUB, dim))
               for j in range(1, block // HGRN_SUB)], axis=0)
        loc = cum - sub_ref
        p = lax.dot_general((q * jnp.exp(loc)).astype(BF16), (k * jnp.exp(-loc)).astype(BF16),
                            contract_last, preferred_element_type=F32)
        scores = p if scores is None else jnp.where(split >= HGRN_SUB, scores, p)
        scores = jnp.where(causal, scores, 0.0)
        intra_ref[rows, :] = jnp.dot(scores.astype(BF16), v16, preferred_element_type=F32)
        q0_ref[rows, :] = (q * jnp.exp(cum)).astype(BF16)
        k_end = (k * jnp.exp(total - cum)).astype(BF16)
        upd_ref[bi] = lax.dot_general(v16, k_end, contract_first, preferred_element_type=F32)
        dec_ref[bi] = jnp.exp(total)
        return c

    lax.fori_loop(0, n_blocks, local, 0, unroll=2)

    def carry_state(bi, state_t):
        st_ref[bi] = state_t.astype(BF16)
        return state_t * dec_ref[bi] + upd_ref[bi]

    lax.fori_loop(0, n_blocks, carry_state, jnp.zeros((dim, dim), F32))

    def finish(bi, c):
        r0 = pl.multiple_of(bi * block, block)
        rows = pl.ds(r0, block)
        o = intra_ref[rows, :] + lax.dot_general(q0_ref[rows, :], st_ref[bi], contract_last,
                                                 preferred_element_type=F32)
        ms = jnp.mean(o * o, axis=-1, keepdims=True)
        y = o * lax.rsqrt(ms + NORM_EPS) * og
        pg = proj_ref[rows, 3 * dim:4 * dim]
        o_ref[0, rows, :] = (y * (pg * _sigmoid(pg))).astype(o_ref.dtype)
        return c

    lax.fori_loop(0, n_blocks, finish, 0, unroll=2)


def _hgrn_mixer(hn, w_heads, lb, out_gain):
    b, s, d = hn.shape
    dim = HGRN_DIM
    block = min(HGRN_BLOCK, s)
    n_blocks = s // block
    return pl.pallas_call(
        functools.partial(_hgrn_kernel, seq=s, block=block),
        grid=(b, HGRN_HEADS),
        in_specs=[pl.BlockSpec((1, s, d), lambda i, h: (i, 0, 0)),
                  pl.BlockSpec((1, d, 4 * dim), lambda i, h: (h, 0, 0)),
                  pl.BlockSpec((1, 1, dim), lambda i, h: (h, 0, 0)),
                  pl.BlockSpec((1, dim), lambda i, h: (0, 0))],
        out_specs=pl.BlockSpec((1, s, dim), lambda i, h: (i, 0, h)),
        out_shape=jax.ShapeDtypeStruct((b, s, d), BF16),
        scratch_shapes=[pltpu.VMEM((s, 4 * dim), F32),
                        pltpu.VMEM((s, dim), F32),
                        pltpu.VMEM((s, dim), F32),
                        pltpu.VMEM((s, dim), BF16),
                        pltpu.VMEM((n_blocks, dim, dim), F32),
                        pltpu.VMEM((n_blocks, 1, dim), F32),
                        pltpu.VMEM((n_blocks, dim, dim), BF16)],
        compiler_params=_params("parallel", "arbitrary"),
        name="hgrn_mixer",
    )(hn, w_heads, lb.reshape(HGRN_HEADS, 1, dim), out_gain.reshape(1, dim).astype(F32))


def _qk_kernel(a_ref, w_ref, g_ref, cos_ref, sin_ref, o_ref, *, tn):
    hd = DIFF_HEAD_DIM
    a = a_ref[...]
    cosf = cos_ref[...]
    sins = sin_ref[...]
    for c0 in range(0, tn, 2 * hd):
        acc = jnp.dot(a, w_ref[:, c0:c0 + 2 * hd], preferred_element_type=F32)
        for hh in range(2):
            x = acc[:, hh * hd:(hh + 1) * hd]
            sl = slice(c0 + hh * hd, c0 + (hh + 1) * hd)
            ms = jnp.mean(x * x, axis=-1, keepdims=True)
            y = x * lax.rsqrt(ms + NORM_EPS) * g_ref[:, sl]
            rot = pltpu.roll(y, hd // 2, 1)
            o_ref[:, sl] = (y * cosf + rot * sins).astype(o_ref.dtype)


def _qk_proj(hn, w, gains, cosf, sins, tm=512, tn=1024):
    n, d = hn.shape
    nout = w.shape[1]
    tm = min(tm, n)
    return pl.pallas_call(
        functools.partial(_qk_kernel, tn=tn),
        grid=(n // tm, nout // tn),
        in_specs=[pl.BlockSpec((tm, d), lambda i, j: (i, 0)),
                  pl.BlockSpec((d, tn), lambda i, j: (0, j)),
                  pl.BlockSpec((1, tn), lambda i, j: (0, j)),
                  pl.BlockSpec((tm, DIFF_HEAD_DIM), lambda i, j: (i, 0)),
                  pl.BlockSpec((tm, DIFF_HEAD_DIM), lambda i, j: (i, 0))],
        out_specs=pl.BlockSpec((tm, tn), lambda i, j: (i, j)),
        out_shape=jax.ShapeDtypeStruct((n, nout), BF16),
        compiler_params=_params("parallel", "arbitrary"),
        name="qk_proj",
    )(hn, w, gains, cosf, sins)


def _mm_bf16_kernel(a_ref, w_ref, o_ref):
    o_ref[...] = jnp.dot(a_ref[...], w_ref[...], preferred_element_type=F32).astype(o_ref.dtype)


def _matmul_bf16(a, w, tm=1024, tn=1024):
    n, k = a.shape
    nout = w.shape[1]
    tm, tn = min(tm, n), min(tn, nout)
    return pl.pallas_call(
        _mm_bf16_kernel,
        grid=(n // tm, nout // tn),
        in_specs=[pl.BlockSpec((tm, k), lambda i, j: (i, 0)), pl.BlockSpec((k, tn), lambda i, j: (0, j))],
        out_specs=pl.BlockSpec((tm, tn), lambda i, j: (i, j)),
        out_shape=jax.ShapeDtypeStruct((n, nout), BF16),
        compiler_params=_params("parallel", "parallel"),
        name="matmul_bf16",
    )(a, w)


def _attn_kernel(q_ref, k_ref, v_ref, lam_ref, sg_ref, o_ref, s_ref, m_ref, l_ref, acc_ref, *, seq, tile, lam_init):
    hd = DIFF_HEAD_DIM
    lf = lam_ref[...]
    lam = (jnp.exp(jnp.sum(lf[0:1] * lf[1:2], axis=-1, keepdims=True))
           - jnp.exp(jnp.sum(lf[2:3] * lf[3:4], axis=-1, keepdims=True)) + lam_init)
    row = lax.broadcasted_iota(jnp.int32, (tile, tile), 0)
    col = lax.broadcasted_iota(jnp.int32, (tile, tile), 1)
    contract_last = (((1,), (1,)), ((), ()))

    def lane_fold(x, op):
        out = x[:, :LANES]
        for t in range(1, tile // LANES):
            out = op(out, x[:, t * LANES:(t + 1) * LANES])
        return out

    def qblock(i, carry):
        q0 = pl.multiple_of(i * tile, tile)

        def score_tile(jb, c2):
            k0 = pl.multiple_of(jb * tile, tile)
            for c in range(2):
                q = q_ref[0, pl.ds(q0, tile), c * hd:(c + 1) * hd]
                kj = k_ref[0, pl.ds(k0, tile), c * hd:(c + 1) * hd]
                s = lax.dot_general(q, kj, contract_last, preferred_element_type=F32)
                s_ref[c, :, pl.ds(k0, tile)] = s
                m_ref[c] = jnp.maximum(m_ref[c], lane_fold(s, jnp.maximum))
            return c2

        for c in range(2):
            m_ref[c] = jnp.full((tile, LANES), -jnp.inf, F32)
        lax.fori_loop(0, i, score_tile, 0)
        for c in range(2):
            q = q_ref[0, pl.ds(q0, tile), c * hd:(c + 1) * hd]
            kd = k_ref[0, pl.ds(q0, tile), c * hd:(c + 1) * hd]
            s = lax.dot_general(q, kd, contract_last, preferred_element_type=F32)
            s = jnp.where(col <= row, s, -jnp.inf)
            s_ref[c, :, pl.ds(q0, tile)] = s
            m_part = jnp.maximum(m_ref[c], lane_fold(s, jnp.maximum))
            m_ref[c] = jnp.broadcast_to(jnp.max(m_part, axis=-1, keepdims=True), (tile, LANES))
            l_ref[c] = jnp.zeros((tile, LANES), F32)
            acc_ref[c] = jnp.zeros((tile, 2 * hd), F32)

        def value_tile(jb, c2):
            k0 = pl.multiple_of(jb * tile, tile)
            vj = v_ref[0, pl.ds(k0, tile), :]
            for c in range(2):
                m = m_ref[c]
                e = jnp.exp(s_ref[c, :, pl.ds(k0, tile)] - jnp.concatenate([m] * (tile // LANES), axis=1))
                l_ref[c] = l_ref[c] + lane_fold(e, jnp.add)
                acc_ref[c] = acc_ref[c] + jnp.dot(e.astype(BF16), vj, preferred_element_type=F32)
            return c2

        lax.fori_loop(0, i + 1, value_tile, 0)
        inv = [1.0 / jnp.sum(l_ref[c], axis=-1, keepdims=True) for c in range(2)]
        o = acc_ref[0] * inv[0] - lam * (acc_ref[1] * inv[1])
        ms = jnp.mean(o * o, axis=-1, keepdims=True)
        y = o * lax.rsqrt(ms + NORM_EPS) * sg_ref[...]
        o_ref[0, pl.ds(q0, tile), :] = (y * (1.0 - lam_init)).astype(o_ref.dtype)
        return carry

    lax.fori_loop(0, seq // tile, qblock, 0)


def _diff_attention(qk, v, lam, sub_gain, lam_init):
    b, s, _ = qk.shape
    hd = DIFF_HEAD_DIM
    tile = min(ATTN_TILE, s)
    nh = DIFF_HEADS
    return pl.pallas_call(
        functools.partial(_attn_kernel, seq=s, tile=tile, lam_init=lam_init),
        grid=(b, nh),
        in_specs=[pl.BlockSpec((1, s, 2 * hd), lambda i, h: (i, 0, h)),
                  pl.BlockSpec((1, s, 2 * hd), lambda i, h: (i, 0, nh + h)),
                  pl.BlockSpec((1, s, 2 * hd), lambda i, h: (i, 0, h)),
                  pl.BlockSpec((4, hd), lambda i, h: (0, 0)),
                  pl.BlockSpec((1, 2 * hd), lambda i, h: (0, 0))],
        out_specs=pl.BlockSpec((1, s, 2 * hd), lambda i, h: (i, 0, h)),
        out_shape=jax.ShapeDtypeStruct((b, s, nh * 2 * hd), BF16),
        scratch_shapes=[pltpu.VMEM((2, tile, s), F32), pltpu.VMEM((2, tile, LANES), F32),
                        pltpu.VMEM((2, tile, LANES), F32), pltpu.VMEM((2, tile, 2 * hd), F32)],
        compiler_params=_params("parallel", "parallel"),
        name="diff_attention",
    )(qk, qk, v, lam.astype(F32), sub_gain.reshape(1, 2 * hd).astype(F32))


def _router_kernel(h_ref, g_ref, w_ref, tri_ref, xn_ref, meta_ref, cnt_ref):
    x = h_ref[...]
    ms = jnp.mean(x * x, axis=-1, keepdims=True)
    xn = x * lax.rsqrt(ms + NORM_EPS) * g_ref[...]
    xn_ref[...] = xn
    x_hi = xn.astype(BF16)
    x_lo = (xn - x_hi.astype(F32)).astype(BF16)
    both = jnp.dot(x_hi, w_ref[...], preferred_element_type=F32)
    lg = both[:, :LANES] + both[:, LANES:] + jnp.dot(x_lo, w_ref[:, :LANES], preferred_element_type=F32)
    lane = lax.broadcasted_iota(jnp.int32, lg.shape, 1)
    is_group = lane < N_GROUPS
    gmax = jnp.max(jnp.where(is_group, lg, -jnp.inf), axis=-1, keepdims=True)
    ge = jnp.where(is_group, jnp.exp(lg - gmax), 0.0)
    gp = ge / jnp.sum(ge, axis=-1, keepdims=True)
    g_top_p = jnp.max(jnp.where(is_group, gp, -1.0), axis=-1, keepdims=True)
    g_top = jnp.min(jnp.where(is_group & (gp == g_top_p), lane, LANES), axis=-1, keepdims=True)
    first = N_GROUPS + g_top * EXPERTS_PER_GROUP
    in_group = (lane >= first) & (lane < first + EXPERTS_PER_GROUP)
    emax = jnp.max(jnp.where(in_group, lg, -jnp.inf), axis=-1, keepdims=True)
    ee = jnp.where(in_group, jnp.exp(lg - emax), 0.0)
    ep = ee / jnp.sum(ee, axis=-1, keepdims=True)
    p1 = jnp.max(jnp.where(in_group, ep, -1.0), axis=-1, keepdims=True)
    i1 = jnp.min(jnp.where(in_group & (ep == p1), lane, LANES), axis=-1, keepdims=True)
    rest = in_group & (lane != i1)
    p2 = jnp.max(jnp.where(rest, ep, -1.0), axis=-1, keepdims=True)
    i2 = jnp.min(jnp.where(rest & (ep == p2), lane, LANES), axis=-1, keepdims=True)
    denom = p1 + p2
    gate1 = g_top_p * p1 / denom
    gate2 = g_top_p * p2 / denom
    e1 = (i1 - N_GROUPS).astype(F32)
    e2 = (i2 - N_GROUPS).astype(F32)

    @pl.when(pl.program_id(0) == 0)
    def _():
        cnt_ref[...] = jnp.zeros_like(cnt_ref)

    pick1 = lane == i1
    pick2 = lane == i2
    chosen = jnp.where(pick1 | pick2, 1.0, 0.0)
    before = jnp.dot(tri_ref[...], chosen.astype(BF16), preferred_element_type=F32) + cnt_ref[...]
    rank1 = jnp.sum(jnp.where(pick1, before, 0.0), axis=-1, keepdims=True)
    rank2 = jnp.sum(jnp.where(pick2, before, 0.0), axis=-1, keepdims=True)
    cnt_ref[...] = cnt_ref[...] + jnp.sum(chosen, axis=0, keepdims=True)
    meta_ref[...] = jnp.where(lane == 0, e1, jnp.where(lane == 1, e2, jnp.where(lane == 2, gate1,
                              jnp.where(lane == 3, gate2, jnp.where(lane == 4, rank1,
                                        jnp.where(lane == 5, rank2, 0.0))))))


def _router(h, gain, w_split, tm=256):
    n, d = h.shape
    tm = min(tm, n)
    tri = jnp.asarray(np.tril(np.ones((tm, tm), np.float32), -1), BF16)
    return pl.pallas_call(
        _router_kernel,
        grid=(n // tm,),
        in_specs=[pl.BlockSpec((tm, d), lambda i: (i, 0)),
                  pl.BlockSpec((1, d), lambda i: (0, 0)),
                  pl.BlockSpec((d, 2 * LANES), lambda i: (0, 0)),
                  pl.BlockSpec((tm, tm), lambda i: (0, 0))],
        out_specs=[pl.BlockSpec((tm, d), lambda i: (i, 0)), pl.BlockSpec((tm, LANES), lambda i: (i, 0)),
                   pl.BlockSpec((1, LANES), lambda i: (0, 0))],
        out_shape=[jax.ShapeDtypeStruct((n, d), F32), jax.ShapeDtypeStruct((n, LANES), F32),
                   jax.ShapeDtypeStruct((1, LANES), F32)],
        compiler_params=_params("arbitrary"),
        name="moe_router",
    )(h, gain.reshape(1, d).astype(F32), w_split, tri)


def _row_copy(src_ref, dst_ref, src_row, dst_row, sem):
    return pltpu.make_async_copy(src_ref.at[pl.ds(src_row, 1)], dst_ref.at[pl.ds(dst_row, 1)], sem)


def _expert_kernel(be_ref, nb_ref, tok_ref, dst_ref, par_ref, nxt_ref,
                   xn_hbm, wg_hbm, wu_hbm, wd_hbm, y_hbm,
                   xbuf, ybuf, xsem, ysem, wg32, wu32, wd32, wsem, wg16, wu16, wd16, *, rows):
    i = pl.program_id(0)
    n_used = nb_ref[0]
    slot = lax.rem(i, 2)
    other = 1 - slot
    stages = ((wg_hbm, wg32), (wu_hbm, wu32), (wd_hbm, wd32))

    def weights(expert, s):
        return [pltpu.make_async_copy(hbm.at[expert], stage.at[s], wsem.at[s]) for hbm, stage in stages]

    def gather(block, into):
        for r in range(rows):
            _row_copy(xn_hbm, xbuf.at[into], tok_ref[block * rows + r], r, xsem.at[into]).start()

    def scatter(block, frm):
        for r in range(rows):
            _row_copy(ybuf.at[frm], y_hbm, r, dst_ref[block * rows + r], ysem.at[frm]).start()

    def wait_gather(s):
        pltpu.make_async_copy(xn_hbm.at[pl.ds(0, rows)], xbuf.at[s], xsem.at[s]).wait()

    def wait_scatter(s):
        pltpu.make_async_copy(ybuf.at[s], y_hbm.at[pl.ds(0, rows)], ysem.at[s]).wait()

    def mlp():
        x = xbuf[slot].astype(BF16)
        g = jnp.dot(x, wg16[...], preferred_element_type=F32)
        u = jnp.dot(x, wu16[...], preferred_element_type=F32)
        hmid = (g * _sigmoid(g)) * u
        ybuf[slot] = jnp.dot(hmid.astype(BF16), wd16[...], preferred_element_type=F32)

    @pl.when(i == 0)
    def _():
        for c in weights(be_ref[0], 0):
            c.start()
        gather(0, 0)
        ybuf[1] = jnp.zeros((rows, ybuf.shape[2]), F32)
        n_real = y_hbm.shape[0] - 2 * rows
        clears = [pltpu.make_async_copy(ybuf.at[1], y_hbm.at[pl.ds(n_real + p * rows, rows)], ysem.at[1])
                  for p in range(2)]
        for c in clears:
            c.start()
        for c in clears:
            c.wait()

    new_expert = jnp.logical_or(i == 0, be_ref[i] != be_ref[jnp.maximum(i - 1, 0)])

    @pl.when(jnp.logical_and(i < n_used, new_expert))
    def _():
        s = par_ref[i]
        for c in weights(0, s):
            c.wait()

        @pl.when(nxt_ref[i] >= 0)
        def _():
            for c in weights(nxt_ref[i], 1 - s):
                c.start()

        wg16[...] = wg32[s].astype(BF16)
        wu16[...] = wu32[s].astype(BF16)
        wd16[...] = wd32[s].astype(BF16)

    @pl.when(i == 0)
    def _():
        wait_gather(slot)
        gather(jnp.minimum(i + 1, n_used - 1), other)
        mlp()

    @pl.when(jnp.logical_and(i >= 1, i < n_used))
    def _():
        @pl.when(i >= 2)
        def _():
            wait_scatter(slot)

        wait_gather(slot)
        gather(jnp.minimum(i + 1, n_used - 1), other)
        scatter(i - 1, other)
        mlp()

    @pl.when(i == n_used)
    def _():
        scatter(i - 1, other)
        wait_scatter(other)

        @pl.when(i >= 2)
        def _():
            wait_scatter(slot)

        wait_gather(slot)


def _expert_mlp(xn, row_tok, row_dst, block_expert, n_used, parity, next_expert, w_gate, w_up, w_down, n_out,
                rows=EXPERT_ROWS):
    d = xn.shape[1]
    n_rows = row_tok.shape[0]
    ff = w_gate.shape[-1]
    any_spec = pl.BlockSpec(memory_space=pl.ANY)
    return pl.pallas_call(
        functools.partial(_expert_kernel, rows=rows),
        grid_spec=pltpu.PrefetchScalarGridSpec(
            num_scalar_prefetch=6,
            grid=(n_rows // rows,),
            in_specs=[any_spec, any_spec, any_spec, any_spec],
            out_specs=any_spec,
            scratch_shapes=[pltpu.VMEM((2, rows, d), F32), pltpu.VMEM((2, rows, d), F32),
                            pltpu.SemaphoreType.DMA((2,)), pltpu.SemaphoreType.DMA((2,)),
                            pltpu.VMEM((2, d, ff), F32), pltpu.VMEM((2, d, ff), F32), pltpu.VMEM((2, ff, d), F32),
                            pltpu.SemaphoreType.DMA((2,)),
                            pltpu.VMEM((d, ff), BF16), pltpu.VMEM((d, ff), BF16), pltpu.VMEM((ff, d), BF16)]),
        out_shape=jax.ShapeDtypeStruct((n_out, d), F32),
        compiler_params=_params("arbitrary"),
        name="moe_experts",
    )(block_expert, n_used, row_tok, row_dst, parity, next_expert, xn, w_gate, w_up, w_down)


def _combine_kernel(*refs, with_norm):
    if with_norm:
        h_ref, meta_ref, y0_ref, y1_ref, g_ref, o_ref, hn_ref = refs
    else:
        h_ref, meta_ref, y0_ref, y1_ref, o_ref = refs
    meta = meta_ref[...]
    out = h_ref[...] + (y0_ref[...] * meta[:, 2:3] + y1_ref[...] * meta[:, 3:4])
    o_ref[...] = out
    if with_norm:
        ms = jnp.mean(out * out, axis=-1, keepdims=True)
        hn_ref[...] = (out * lax.rsqrt(ms + NORM_EPS) * g_ref[...]).astype(hn_ref.dtype)


def _combine(h, meta, y2, next_gain=None, tm=256):
    n, d = h.shape
    tm = min(tm, n)
    with_norm = next_gain is not None
    row_spec = pl.BlockSpec((tm, d), lambda i: (i, 0))
    in_specs = [row_spec, pl.BlockSpec((tm, LANES), lambda i: (i, 0)), row_spec,
                pl.BlockSpec((tm, d), lambda i: (n // tm + i, 0))]
    args = [h, meta, y2, y2]
    out_specs, out_shape = [row_spec], [jax.ShapeDtypeStruct((n, d), F32)]
    if with_norm:
        in_specs.append(pl.BlockSpec((1, d), lambda i: (0, 0)))
        args.append(next_gain.reshape(1, d).astype(F32))
        out_specs.append(row_spec)
        out_shape.append(jax.ShapeDtypeStruct((n, d), BF16))
    outs = pl.pallas_call(
        functools.partial(_combine_kernel, with_norm=with_norm),
        grid=(n // tm,),
        in_specs=in_specs,
        out_specs=out_specs,
        out_shape=out_shape,
        compiler_params=_params("parallel"),
        name="moe_combine",
    )(*args)
    return (outs[0], outs[1]) if with_norm else (outs[0], None)


def _split_router_weights(w_group, w_expert):
    d = w_group.shape[0]
    w = jnp.concatenate([w_group.astype(F32), w_expert.astype(F32),
                         jnp.zeros((d, LANES - N_GROUPS - N_EXPERTS), F32)], axis=1)
    hi = w.astype(BF16)
    lo = (w - hi.astype(F32)).astype(BF16)
    return jnp.concatenate([hi, lo], axis=1)


def _hier_moe(h, ffn_gain, w_group, w_expert, w_gate, w_up, w_down, layer, next_gain=None):
    n, d = h.shape
    rows = EXPERT_ROWS
    xn, meta, cnt = _router(h, ffn_gain, _split_router_weights(w_group, w_expert))
    ids = jnp.arange(N_EXPERTS, dtype=jnp.int32)
    expert = meta[:, :TOP_K].astype(jnp.int32)
    rank = meta[:, 4:4 + TOP_K].astype(jnp.int32)
    counts = cnt[0, N_GROUPS:N_GROUPS + N_EXPERTS].astype(jnp.int32)
    n_assign = n * TOP_K
    n_blocks = n_assign // rows + N_EXPERTS
    n_rows = n_blocks * rows
    pad_counts = (counts + rows - 1) // rows * rows
    pad_end = jnp.cumsum(pad_counts)
    pad_start = pad_end - pad_counts
    slot = pad_start[expert] + rank
    payload = jnp.arange(TOP_K, dtype=jnp.int32)[None, :] * n + jnp.arange(n, dtype=jnp.int32)[:, None]
    payload = jnp.full((n_rows,), -1, jnp.int32).at[slot.reshape(-1)].set(payload.reshape(-1))
    r = jnp.arange(n_rows, dtype=jnp.int32)
    spare = TOP_K * n + (r // rows % 2) * rows + r % rows
    row_tok = jnp.where(payload < 0, 0, payload % n)
    row_dst = jnp.where(payload < 0, spare, payload)
    n_used = (pad_end[-1] // rows).astype(jnp.int32).reshape(1)
    blk = jnp.minimum(jnp.arange(n_blocks, dtype=jnp.int32), n_used[0] - 1)
    block_expert = jnp.sum((pad_end[None, :] <= (blk * rows)[:, None]).astype(jnp.int32), axis=1)
    block_expert = jnp.minimum(block_expert, N_EXPERTS - 1)
    nonempty = counts > 0
    position = jnp.cumsum(nonempty.astype(jnp.int32)) - 1
    later = jnp.where(nonempty[None, :] & (ids[None, :] > ids[:, None]), ids[None, :], N_EXPERTS)
    following = jnp.min(later, axis=1)
    following = jnp.where(following < N_EXPERTS, following + layer * N_EXPERTS, -1)
    y2 = _expert_mlp(xn, row_tok, row_dst, block_expert + layer * N_EXPERTS, n_used,
                     position[block_expert] % 2, following[block_expert], w_gate, w_up, w_down,
                     TOP_K * n + 2 * rows, rows)
    return _combine(h, meta, y2, next_gain)


def kernel(x, positions, mix_norm, ffn_norm, hgrn_w_in, hgrn_out_gain, hgrn_w_out, hgrn_lb_logits, diff_w_in, diff_q_gain, diff_k_gain, diff_lambda, diff_sub_gain, diff_w_out, moe_w_group, moe_w_expert, moe_w_gate, moe_w_up, moe_w_down):
    b, s, d = x.shape
    n = b * s
    depth = mix_norm.shape[0]
    lower_bounds = jnp.cumsum(jax.nn.softmax(hgrn_lb_logits.astype(F32), axis=0), axis=0)
    h = x.reshape(n, d)
    w_gate = moe_w_gate.reshape((-1,) + moe_w_gate.shape[2:])
    w_up = moe_w_up.reshape((-1,) + moe_w_up.shape[2:])
    w_down = moe_w_down.reshape((-1,) + moe_w_down.shape[2:])
    hn = _rmsnorm_bf16(h, mix_norm[0])
    for layer in range(depth):
        j = layer // N_MIXERS
        if layer % N_MIXERS == 0:
            w_heads = (hgrn_w_in[j].reshape(d, 4, HGRN_HEADS, HGRN_DIM).transpose(2, 0, 1, 3)
                       .reshape(HGRN_HEADS, d, 4 * HGRN_DIM).astype(BF16))
            o = _hgrn_mixer(hn.reshape(b, s, d), w_heads, lower_bounds[layer], hgrn_out_gain[j])
            h = _matmul_residual(o.reshape(n, d), hgrn_w_out[j].astype(BF16), h)
        else:
            lam_init = 0.8 - 0.6 * math.exp(-0.3 * layer)
            hd = DIFF_HEAD_DIM
            half = hd // 2
            inv_freq = ROPE_THETA ** (-jnp.arange(half, dtype=F32) / half)
            ang = positions.astype(F32).reshape(n, 1) * inv_freq[None, :]
            cosf = jnp.concatenate([jnp.cos(ang), jnp.cos(ang)], axis=1)
            sins = jnp.concatenate([-jnp.sin(ang), jnp.sin(ang)], axis=1)
            n_heads = 2 * DIFF_HEADS
            n_qk = 2 * n_heads * hd
            gains = jnp.concatenate([jnp.tile(diff_q_gain[j].astype(F32), n_heads) * (hd ** -0.5),
                                     jnp.tile(diff_k_gain[j].astype(F32), n_heads)]).reshape(1, -1)
            w_in = diff_w_in[j].astype(BF16)
            qk = _qk_proj(hn, w_in[:, :n_qk], gains, cosf, sins)
            v = _matmul_bf16(hn, w_in[:, n_qk:])
            o = _diff_attention(qk.reshape(b, s, -1), v.reshape(b, s, -1), diff_lambda[j], diff_sub_gain[j],
                                lam_init)
            h = _matmul_residual(o.reshape(n, d), diff_w_out[j].astype(BF16), h)
        next_gain = mix_norm[layer + 1] if layer + 1 < depth else None
        h, hn = _hier_moe(h, ffn_norm[layer], moe_w_group[layer], moe_w_expert[layer], w_gate, w_up, w_down, layer,
                          next_gain)
    return h.reshape(b, s, d)
```

```python
import functools
import math

import jax
import jax.numpy as jnp
import numpy as np
from jax import lax
from jax.experimental import pallas as pl
from jax.experimental.pallas import tpu as pltpu

F32 = jnp.float32
BF16 = jnp.bfloat16

NORM_EPS = 1e-6
N_MIXERS = 2
HGRN_HEADS = 16
HGRN_DIM = 128
HGRN_BLOCK = 128
HGRN_SUB = 16
DIFF_HEADS = 8
DIFF_HEAD_DIM = 128
ROPE_THETA = 10000.0
ATTN_TILE = 256
N_GROUPS = 8
EXPERTS_PER_GROUP = 8
N_EXPERTS = N_GROUPS * EXPERTS_PER_GROUP
TOP_K = 2
EXPERT_ROWS = 256
GATHER_SLOTS = 3
ROW_DMA_PRIORITY = 0
WEIGHT_DMA_PRIORITY = 1
LANES = 128
VMEM_LIMIT_BYTES = 56 * 1024 * 1024


def _params(*semantics):
    return pltpu.CompilerParams(dimension_semantics=semantics, vmem_limit_bytes=VMEM_LIMIT_BYTES)


def _norm_kernel(h_ref, g_ref, o_ref):
    x = h_ref[...]
    ms = jnp.mean(x * x, axis=-1, keepdims=True)
    o_ref[...] = (x * lax.rsqrt(ms + NORM_EPS) * g_ref[...]).astype(o_ref.dtype)


def _rmsnorm_bf16(h, gain, tm=512):
    n, d = h.shape
    return pl.pallas_call(
        _norm_kernel,
        grid=(n // tm,),
        in_specs=[pl.BlockSpec((tm, d), lambda i: (i, 0)), pl.BlockSpec((1, d), lambda i: (0, 0))],
        out_specs=pl.BlockSpec((tm, d), lambda i: (i, 0)),
        out_shape=jax.ShapeDtypeStruct((n, d), BF16),
        compiler_params=_params("parallel"),
        name="rmsnorm_bf16",
    )(h, gain.reshape(1, d).astype(F32))


def _mm_res_kernel(a_ref, w_ref, r_ref, o_ref):
    o_ref[...] = r_ref[...] + jnp.dot(a_ref[...], w_ref[...], preferred_element_type=F32)


def _matmul_residual(a, w, res, tm=1024, tn=1024):
    n, k = a.shape
    nout = w.shape[1]
    tm, tn = min(tm, n), min(tn, nout)
    return pl.pallas_call(
        _mm_res_kernel,
        grid=(n // tm, nout // tn),
        in_specs=[pl.BlockSpec((tm, k), lambda i, j: (i, 0)),
                  pl.BlockSpec((k, tn), lambda i, j: (0, j)),
                  pl.BlockSpec((tm, tn), lambda i, j: (i, j))],
        out_specs=pl.BlockSpec((tm, tn), lambda i, j: (i, j)),
        out_shape=jax.ShapeDtypeStruct((n, nout), F32),
        compiler_params=_params("parallel", "parallel"),
        name="matmul_residual",
    )(a, w, res)


def _sigmoid(x):
    return 1.0 / (1.0 + jnp.exp(-x))


def _ref_rows(a_ref, half, block):
    span = 2 * half
    parts = [jnp.broadcast_to(a_ref[g * span + half - 1:g * span + half, :], (span, a_ref.shape[1]))
             for g in range(block // span)]
    return parts[0] if len(parts) == 1 else jnp.concatenate(parts, axis=0)


def _cumsum_rows(x, row):
    shift = 1
    while shift < x.shape[0]:
        x = x + jnp.where(row >= shift, pltpu.roll(x, shift, 0), 0.0)
        shift *= 2
    return x


def _hgrn_kernel(hn_ref, w_ref, lb_ref, og_ref, o_ref, proj_ref, cum_ref, intra_ref, q0_ref, upd_ref, dec_ref,
                 st_ref, *, seq, block):
    dim = HGRN_DIM
    n_blocks = seq // block
    proj_ref[...] = jnp.dot(hn_ref[0], w_ref[0], preferred_element_type=F32)
    lb = lb_ref[0]
    og = og_ref[...]
    t_idx = lax.broadcasted_iota(jnp.int32, (block, block), 0)
    s_idx = lax.broadcasted_iota(jnp.int32, (block, block), 1)
    causal = s_idx <= t_idx
    split = t_idx ^ s_idx
    row = lax.broadcasted_iota(jnp.int32, (block, dim), 0)
    halves = []
    half = block // 2
    while half >= HGRN_SUB:
        halves.append(half)
        half //= 2
    contract_last = (((1,), (1,)), ((), ()))
    contract_first = (((0,), (0,)), ((), ()))

    def local(bi, c):
        r0 = pl.multiple_of(bi * block, block)
        rows = pl.ds(r0, block)
        pq = proj_ref[rows, 0 * dim:1 * dim]
        pf = proj_ref[rows, 1 * dim:2 * dim]
        v16 = proj_ref[rows, 2 * dim:3 * dim].astype(BF16)
        q = pq * _sigmoid(pq)
        forget = lb + (1.0 - lb) * _sigmoid(pf)
        k = 1.0 - forget
        cum = _cumsum_rows(jnp.log(forget), row)
        cum_ref[rows, :] = cum
        blk = cum_ref.at[rows]
        total = blk[block - 1:block, :]
        scores = None
        for half in halves:
            e = jnp.exp(-jnp.abs(cum - _ref_rows(blk, half, block)))
            p = lax.dot_general((q * e).astype(BF16), (k * e).astype(BF16), contract_last,
                                preferred_element_type=F32)
            scores = p if scores is None else jnp.where(split >= 2 * half, scores, p)
        sub_ref = jnp.concatenate(
            [jnp.zeros((HGRN_SUB, dim), F32)]
            + [jnp.broadcast_to(blk[j * HGRN_SUB - 1:j * HGRN_SUB, :], (HGRN_SUB, dim))
               for j in range(1, block // HGRN_SUB)], axis=0)
        loc = cum - sub_ref
        p = lax.dot_general((q * jnp.exp(loc)).astype(BF16), (k * jnp.exp(-loc)).astype(BF16),
                            contract_last, preferred_element_type=F32)
        scores = p if scores is None else jnp.where(split >= HGRN_SUB, scores, p)
        scores = jnp.where(causal, scores, 0.0)
        intra_ref[rows, :] = jnp.dot(scores.astype(BF16), v16, preferred_element_type=F32)
        q0_ref[rows, :] = (q * jnp.exp(cum)).astype(BF16)
        k_end = (k * jnp.exp(total - cum)).astype(BF16)
        upd_ref[bi] = lax.dot_general(v16, k_end, contract_first, preferred_element_type=F32)
        dec_ref[bi] = jnp.exp(total)
        return c

    lax.fori_loop(0, n_blocks, local, 0, unroll=2)

    def carry_state(bi, state_t):
        st_ref[bi] = state_t.astype(BF16)
        return state_t * dec_ref[bi] + upd_ref[bi]

    lax.fori_loop(0, n_blocks, carry_state, jnp.zeros((dim, dim), F32))

    def finish(bi, c):
        r0 = pl.multiple_of(bi * block, block)
        rows = pl.ds(r0, block)
        o = intra_ref[rows, :] + lax.dot_general(q0_ref[rows, :], st_ref[bi], contract_last,
                                                 preferred_element_type=F32)
        ms = jnp.mean(o * o, axis=-1, keepdims=True)
        y = o * lax.rsqrt(ms + NORM_EPS) * og
        pg = proj_ref[rows, 3 * dim:4 * dim]
        o_ref[0, rows, :] = (y * (pg * _sigmoid(pg))).astype(o_ref.dtype)
        return c

    lax.fori_loop(0, n_blocks, finish, 0, unroll=min(4, n_blocks))


def _hgrn_mixer(hn, w_heads, lb, out_gain):
    b, s, d = hn.shape
    dim = HGRN_DIM
    block = min(HGRN_BLOCK, s)
    n_blocks = s // block
    return pl.pallas_call(
        functools.partial(_hgrn_kernel, seq=s, block=block),
        grid=(b, HGRN_HEADS),
        in_specs=[pl.BlockSpec((1, s, d), lambda i, h: (i, 0, 0)),
                  pl.BlockSpec((1, d, 4 * dim), lambda i, h: (h, 0, 0)),
                  pl.BlockSpec((1, 1, dim), lambda i, h: (h, 0, 0)),
                  pl.BlockSpec((1, dim), lambda i, h: (0, 0))],
        out_specs=pl.BlockSpec((1, s, dim), lambda i, h: (i, 0, h)),
        out_shape=jax.ShapeDtypeStruct((b, s, d), BF16),
        scratch_shapes=[pltpu.VMEM((s, 4 * dim), F32),
                        pltpu.VMEM((s, dim), F32),
                        pltpu.VMEM((s, dim), F32),
                        pltpu.VMEM((s, dim), BF16),
                        pltpu.VMEM((n_blocks, dim, dim), F32),
                        pltpu.VMEM((n_blocks, 1, dim), F32),
                        pltpu.VMEM((n_blocks, dim, dim), BF16)],
        compiler_params=_params("parallel", "arbitrary"),
        name="hgrn_mixer",
    )(hn, w_heads, lb.reshape(HGRN_HEADS, 1, dim), out_gain.reshape(1, dim).astype(F32))


def _qk_kernel(a_ref, w_ref, g_ref, cos_ref, sin_ref, o_ref, *, tn):
    hd = DIFF_HEAD_DIM
    a = a_ref[...]
    cosf = cos_ref[...]
    sins = sin_ref[...]
    for c0 in range(0, tn, 2 * hd):
        acc = jnp.dot(a, w_ref[:, c0:c0 + 2 * hd], preferred_element_type=F32)
        for hh in range(2):
            x = acc[:, hh * hd:(hh + 1) * hd]
            sl = slice(c0 + hh * hd, c0 + (hh + 1) * hd)
            ms = jnp.mean(x * x, axis=-1, keepdims=True)
            y = x * lax.rsqrt(ms + NORM_EPS) * g_ref[:, sl]
            rot = pltpu.roll(y, hd // 2, 1)
            o_ref[:, sl] = (y * cosf + rot * sins).astype(o_ref.dtype)


def _qk_proj(hn, w, gains, cosf, sins, tm=512, tn=1024):
    n, d = hn.shape
    nout = w.shape[1]
    tm = min(tm, n)
    return pl.pallas_call(
        functools.partial(_qk_kernel, tn=tn),
        grid=(n // tm, nout // tn),
        in_specs=[pl.BlockSpec((tm, d), lambda i, j: (i, 0)),
                  pl.BlockSpec((d, tn), lambda i, j: (0, j)),
                  pl.BlockSpec((1, tn), lambda i, j: (0, j)),
                  pl.BlockSpec((tm, DIFF_HEAD_DIM), lambda i, j: (i, 0)),
                  pl.BlockSpec((tm, DIFF_HEAD_DIM), lambda i, j: (i, 0))],
        out_specs=pl.BlockSpec((tm, tn), lambda i, j: (i, j)),
        out_shape=jax.ShapeDtypeStruct((n, nout), BF16),
        compiler_params=_params("parallel", "arbitrary"),
        name="qk_proj",
    )(hn, w, gains, cosf, sins)


def _mm_bf16_kernel(a_ref, w_ref, o_ref):
    o_ref[...] = jnp.dot(a_ref[...], w_ref[...], preferred_element_type=F32).astype(o_ref.dtype)


def _matmul_bf16(a, w, tm=1024, tn=1024):
    n, k = a.shape
    nout = w.shape[1]
    tm, tn = min(tm, n), min(tn, nout)
    return pl.pallas_call(
        _mm_bf16_kernel,
        grid=(n // tm, nout // tn),
        in_specs=[pl.BlockSpec((tm, k), lambda i, j: (i, 0)), pl.BlockSpec((k, tn), lambda i, j: (0, j))],
        out_specs=pl.BlockSpec((tm, tn), lambda i, j: (i, j)),
        out_shape=jax.ShapeDtypeStruct((n, nout), BF16),
        compiler_params=_params("parallel", "parallel"),
        name="matmul_bf16",
    )(a, w)


def _attn_kernel(q_ref, k_ref, v_ref, lam_ref, sg_ref, o_ref, s_ref, m_ref, l_ref, acc_ref, *, seq, tile, lam_init):
    hd = DIFF_HEAD_DIM
    lf = lam_ref[...]
    lam = (jnp.exp(jnp.sum(lf[0:1] * lf[1:2], axis=-1, keepdims=True))
           - jnp.exp(jnp.sum(lf[2:3] * lf[3:4], axis=-1, keepdims=True)) + lam_init)
    row = lax.broadcasted_iota(jnp.int32, (tile, tile), 0)
    col = lax.broadcasted_iota(jnp.int32, (tile, tile), 1)
    contract_last = (((1,), (1,)), ((), ()))

    def lane_fold(x, op):
        out = x[:, :LANES]
        for t in range(1, tile // LANES):
            out = op(out, x[:, t * LANES:(t + 1) * LANES])
        return out

    def qblock(i, carry):
        q0 = pl.multiple_of(i * tile, tile)

        def score_tile(jb, c2):
            k0 = pl.multiple_of(jb * tile, tile)
            for c in range(2):
                q = q_ref[0, pl.ds(q0, tile), c * hd:(c + 1) * hd]
                kj = k_ref[0, pl.ds(k0, tile), c * hd:(c + 1) * hd]
                s = lax.dot_general(q, kj, contract_last, preferred_element_type=F32)
                s_ref[c, :, pl.ds(k0, tile)] = s
                m_ref[c] = jnp.maximum(m_ref[c], lane_fold(s, jnp.maximum))
            return c2

        for c in range(2):
            m_ref[c] = jnp.full((tile, LANES), -jnp.inf, F32)
        lax.fori_loop(0, i, score_tile, 0)
        for c in range(2):
            q = q_ref[0, pl.ds(q0, tile), c * hd:(c + 1) * hd]
            kd = k_ref[0, pl.ds(q0, tile), c * hd:(c + 1) * hd]
            s = lax.dot_general(q, kd, contract_last, preferred_element_type=F32)
            s = jnp.where(col <= row, s, -jnp.inf)
            s_ref[c, :, pl.ds(q0, tile)] = s
            m_part = jnp.maximum(m_ref[c], lane_fold(s, jnp.maximum))
            m_ref[c] = jnp.broadcast_to(jnp.max(m_part, axis=-1, keepdims=True), (tile, LANES))
            l_ref[c] = jnp.zeros((tile, LANES), F32)
            acc_ref[c] = jnp.zeros((tile, 2 * hd), F32)

        def value_tile(jb, c2):
            k0 = pl.multiple_of(jb * tile, tile)
            vj = v_ref[0, pl.ds(k0, tile), :]
            for c in range(2):
                m = m_ref[c]
                e = jnp.exp(s_ref[c, :, pl.ds(k0, tile)] - jnp.concatenate([m] * (tile // LANES), axis=1))
                l_ref[c] = l_ref[c] + lane_fold(e, jnp.add)
                acc_ref[c] = acc_ref[c] + jnp.dot(e.astype(BF16), vj, preferred_element_type=F32)
            return c2

        lax.fori_loop(0, i + 1, value_tile, 0)
        inv = [1.0 / jnp.sum(l_ref[c], axis=-1, keepdims=True) for c in range(2)]
        o = acc_ref[0] * inv[0] - lam * (acc_ref[1] * inv[1])
        ms = jnp.mean(o * o, axis=-1, keepdims=True)
        y = o * lax.rsqrt(ms + NORM_EPS) * sg_ref[...]
        o_ref[0, pl.ds(q0, tile), :] = (y * (1.0 - lam_init)).astype(o_ref.dtype)
        return carry

    lax.fori_loop(0, seq // tile, qblock, 0)


def _diff_attention(qk, v, lam, sub_gain, lam_init):
    b, s, _ = qk.shape
    hd = DIFF_HEAD_DIM
    tile = min(ATTN_TILE, s)
    nh = DIFF_HEADS
    return pl.pallas_call(
        functools.partial(_attn_kernel, seq=s, tile=tile, lam_init=lam_init),
        grid=(b, nh),
        in_specs=[pl.BlockSpec((1, s, 2 * hd), lambda i, h: (i, 0, h)),
                  pl.BlockSpec((1, s, 2 * hd), lambda i, h: (i, 0, nh + h)),
                  pl.BlockSpec((1, s, 2 * hd), lambda i, h: (i, 0, h)),
                  pl.BlockSpec((4, hd), lambda i, h: (0, 0)),
                  pl.BlockSpec((1, 2 * hd), lambda i, h: (0, 0))],
        out_specs=pl.BlockSpec((1, s, 2 * hd), lambda i, h: (i, 0, h)),
        out_shape=jax.ShapeDtypeStruct((b, s, nh * 2 * hd), BF16),
        scratch_shapes=[pltpu.VMEM((2, tile, s), F32), pltpu.VMEM((2, tile, LANES), F32),
                        pltpu.VMEM((2, tile, LANES), F32), pltpu.VMEM((2, tile, 2 * hd), F32)],
        compiler_params=_params("parallel", "parallel"),
        name="diff_attention",
    )(qk, qk, v, lam.astype(F32), sub_gain.reshape(1, 2 * hd).astype(F32))


def _router_kernel(h_ref, g_ref, w_ref, xn_ref, meta_ref, cnt_ref):
    x = h_ref[...]
    ms = jnp.mean(x * x, axis=-1, keepdims=True)
    xn = x * lax.rsqrt(ms + NORM_EPS) * g_ref[...]
    xn_ref[...] = xn
    x_hi = xn.astype(BF16)
    x_lo = (xn - x_hi.astype(F32)).astype(BF16)
    both = jnp.dot(x_hi, w_ref[...], preferred_element_type=F32)
    lg = both[:, :LANES] + both[:, LANES:] + jnp.dot(x_lo, w_ref[:, :LANES], preferred_element_type=F32)
    lane = lax.broadcasted_iota(jnp.int32, lg.shape, 1)
    is_group = lane < N_GROUPS
    gmax = jnp.max(jnp.where(is_group, lg, -jnp.inf), axis=-1, keepdims=True)
    ge = jnp.where(is_group, jnp.exp(lg - gmax), 0.0)
    gp = ge / jnp.sum(ge, axis=-1, keepdims=True)
    g_top_p = jnp.max(jnp.where(is_group, gp, -1.0), axis=-1, keepdims=True)
    g_top = jnp.min(jnp.where(is_group & (gp == g_top_p), lane, LANES), axis=-1, keepdims=True)
    first = N_GROUPS + g_top * EXPERTS_PER_GROUP
    in_group = (lane >= first) & (lane < first + EXPERTS_PER_GROUP)
    emax = jnp.max(jnp.where(in_group, lg, -jnp.inf), axis=-1, keepdims=True)
    ee = jnp.where(in_group, jnp.exp(lg - emax), 0.0)
    ep = ee / jnp.sum(ee, axis=-1, keepdims=True)
    p1 = jnp.max(jnp.where(in_group, ep, -1.0), axis=-1, keepdims=True)
    i1 = jnp.min(jnp.where(in_group & (ep == p1), lane, LANES), axis=-1, keepdims=True)
    rest = in_group & (lane != i1)
    p2 = jnp.max(jnp.where(rest, ep, -1.0), axis=-1, keepdims=True)
    i2 = jnp.min(jnp.where(rest & (ep == p2), lane, LANES), axis=-1, keepdims=True)
    denom = p1 + p2
    gate1 = g_top_p * p1 / denom
    gate2 = g_top_p * p2 / denom
    e1 = (i1 - N_GROUPS).astype(F32)
    e2 = (i2 - N_GROUPS).astype(F32)

    @pl.when(pl.program_id(0) == 0)
    def _():
        cnt_ref[...] = jnp.zeros_like(cnt_ref)

    chosen = jnp.where((lane == i1) | (lane == i2), 1.0, 0.0)
    cnt_ref[...] = cnt_ref[...] + jnp.sum(chosen, axis=0, keepdims=True)
    meta_ref[...] = jnp.where(lane == 0, e1, jnp.where(lane == 1, e2, jnp.where(lane == 2, gate1,
                              jnp.where(lane == 3, gate2, 0.0))))


def _router(h, gain, w_split, tm=256):
    n, d = h.shape
    tm = min(tm, n)
    return pl.pallas_call(
        _router_kernel,
        grid=(n // tm,),
        in_specs=[pl.BlockSpec((tm, d), lambda i: (i, 0)),
                  pl.BlockSpec((1, d), lambda i: (0, 0)),
                  pl.BlockSpec((d, 2 * LANES), lambda i: (0, 0))],
        out_specs=[pl.BlockSpec((tm, d), lambda i: (i, 0)), pl.BlockSpec((tm, LANES), lambda i: (i, 0)),
                   pl.BlockSpec((1, LANES), lambda i: (0, 0))],
        out_shape=[jax.ShapeDtypeStruct((n, d), F32), jax.ShapeDtypeStruct((n, LANES), F32),
                   jax.ShapeDtypeStruct((1, LANES), F32)],
        compiler_params=_params("arbitrary"),
        name="moe_router",
    )(h, gain.reshape(1, d).astype(F32), w_split)


def _row_copy(src_ref, dst_ref, src_row, dst_row, sem):
    return pltpu.make_async_copy(src_ref.at[pl.ds(src_row, 1)], dst_ref.at[pl.ds(dst_row, 1)], sem)


def _expert_kernel(be_ref, nb_ref, tok_ref, dst_ref, par_ref, nxt_ref,
                   xn_hbm, wg_hbm, wu_hbm, wd_hbm, y_hbm,
                   xbuf, ybuf, xsem, ysem, wg32, wu32, wd32, wsem, wg16, wu16, wd16, *, rows):
    i = pl.program_id(0)
    n_used = nb_ref[0]
    slot = lax.rem(i, 2)
    other = 1 - slot
    xslot = lax.rem(i, GATHER_SLOTS)
    stages = ((wg_hbm, wg32), (wu_hbm, wu32), (wd_hbm, wd32))

    def weights(expert, s):
        return [pltpu.make_async_copy(hbm.at[expert], stage.at[s], wsem.at[s]) for hbm, stage in stages]

    def gather(block, into):
        for r in range(rows):
            _row_copy(xn_hbm, xbuf.at[into], tok_ref[block * rows + r], r, xsem.at[into]).start(ROW_DMA_PRIORITY)

    def scatter(block, frm):
        for r in range(rows):
            _row_copy(ybuf.at[frm], y_hbm, r, dst_ref[block * rows + r], ysem.at[frm]).start(ROW_DMA_PRIORITY)

    def wait_gather(s):
        pltpu.make_async_copy(xn_hbm.at[pl.ds(0, rows)], xbuf.at[s], xsem.at[s]).wait()

    def wait_scatter(s):
        pltpu.make_async_copy(ybuf.at[s], y_hbm.at[pl.ds(0, rows)], ysem.at[s]).wait()

    def ahead(k):
        return jnp.minimum(i + k, n_used - 1)

    def mlp():
        x = xbuf[xslot].astype(BF16)
        g = jnp.dot(x, wg16[...], preferred_element_type=F32)
        u = jnp.dot(x, wu16[...], preferred_element_type=F32)
        hmid = (g * _sigmoid(g)) * u
        ybuf[slot] = jnp.dot(hmid.astype(BF16), wd16[...], preferred_element_type=F32)

    @pl.when(i == 0)
    def _():
        for c in weights(be_ref[0], 0):
            c.start(WEIGHT_DMA_PRIORITY)
        for k in range(GATHER_SLOTS - 1):
            gather(ahead(k), k)
        ybuf[1] = jnp.zeros((rows, ybuf.shape[2]), F32)
        n_real = y_hbm.shape[0] - 2 * rows
        clears = [pltpu.make_async_copy(ybuf.at[1], y_hbm.at[pl.ds(n_real + p * rows, rows)], ysem.at[1])
                  for p in range(2)]
        for c in clears:
            c.start()
        for c in clears:
            c.wait()

    new_expert = jnp.logical_or(i == 0, be_ref[i] != be_ref[jnp.maximum(i - 1, 0)])

    @pl.when(jnp.logical_and(i < n_used, new_expert))
    def _():
        s = par_ref[i]
        for c in weights(0, s):
            c.wait()

        @pl.when(nxt_ref[i] >= 0)
        def _():
            for c in weights(nxt_ref[i], 1 - s):
                c.start(WEIGHT_DMA_PRIORITY)

        wg16[...] = wg32[s].astype(BF16)
        wu16[...] = wu32[s].astype(BF16)
        wd16[...] = wd32[s].astype(BF16)

    @pl.when(i == 0)
    def _():
        wait_gather(xslot)
        gather(ahead(GATHER_SLOTS - 1), lax.rem(i + GATHER_SLOTS - 1, GATHER_SLOTS))
        mlp()

    @pl.when(jnp.logical_and(i >= 1, i < n_used))
    def _():
        @pl.when(i >= 2)
        def _():
            wait_scatter(slot)

        wait_gather(xslot)
        gather(ahead(GATHER_SLOTS - 1), lax.rem(i + GATHER_SLOTS - 1, GATHER_SLOTS))
        scatter(i - 1, other)
        mlp()

    @pl.when(i == n_used)
    def _():
        scatter(i - 1, other)
        wait_scatter(other)

        @pl.when(i >= 2)
        def _():
            wait_scatter(slot)

        for k in range(GATHER_SLOTS - 1):
            wait_gather(lax.rem(i + k, GATHER_SLOTS))


def _expert_mlp(xn, row_tok, row_dst, block_expert, n_used, parity, next_expert, w_gate, w_up, w_down, n_out,
                rows=EXPERT_ROWS):
    d = xn.shape[1]
    n_rows = row_tok.shape[0]
    ff = w_gate.shape[-1]
    any_spec = pl.BlockSpec(memory_space=pl.ANY)
    return pl.pallas_call(
        functools.partial(_expert_kernel, rows=rows),
        grid_spec=pltpu.PrefetchScalarGridSpec(
            num_scalar_prefetch=6,
            grid=(n_rows // rows,),
            in_specs=[any_spec, any_spec, any_spec, any_spec],
            out_specs=any_spec,
            scratch_shapes=[pltpu.VMEM((GATHER_SLOTS, rows, d), F32), pltpu.VMEM((2, rows, d), F32),
                            pltpu.SemaphoreType.DMA((GATHER_SLOTS,)), pltpu.SemaphoreType.DMA((2,)),
                            pltpu.VMEM((2, d, ff), F32), pltpu.VMEM((2, d, ff), F32), pltpu.VMEM((2, ff, d), F32),
                            pltpu.SemaphoreType.DMA((2,)),
                            pltpu.VMEM((d, ff), BF16), pltpu.VMEM((d, ff), BF16), pltpu.VMEM((ff, d), BF16)]),
        out_shape=jax.ShapeDtypeStruct((n_out, d), F32),
        compiler_params=_params("arbitrary"),
        name="moe_experts",
    )(block_expert, n_used, row_tok, row_dst, parity, next_expert, xn, w_gate, w_up, w_down)


def _combine_kernel(*refs, with_norm):
    if with_norm:
        h_ref, meta_ref, y0_ref, y1_ref, g_ref, o_ref, hn_ref = refs
    else:
        h_ref, meta_ref, y0_ref, y1_ref, o_ref = refs
    meta = meta_ref[...]
    out = h_ref[...] + (y0_ref[...] * meta[:, 2:3] + y1_ref[...] * meta[:, 3:4])
    o_ref[...] = out
    if with_norm:
        ms = jnp.mean(out * out, axis=-1, keepdims=True)
        hn_ref[...] = (out * lax.rsqrt(ms + NORM_EPS) * g_ref[...]).astype(hn_ref.dtype)


def _combine(h, meta, y2, next_gain=None, tm=256):
    n, d = h.shape
    tm = min(tm, n)
    with_norm = next_gain is not None
    row_spec = pl.BlockSpec((tm, d), lambda i: (i, 0))
    in_specs = [row_spec, pl.BlockSpec((tm, LANES), lambda i: (i, 0)), row_spec,
                pl.BlockSpec((tm, d), lambda i: (n // tm + i, 0))]
    args = [h, meta, y2, y2]
    out_specs, out_shape = [row_spec], [jax.ShapeDtypeStruct((n, d), F32)]
    if with_norm:
        in_specs.append(pl.BlockSpec((1, d), lambda i: (0, 0)))
        args.append(next_gain.reshape(1, d).astype(F32))
        out_specs.append(row_spec)
        out_shape.append(jax.ShapeDtypeStruct((n, d), BF16))
    outs = pl.pallas_call(
        functools.partial(_combine_kernel, with_norm=with_norm),
        grid=(n // tm,),
        in_specs=in_specs,
        out_specs=out_specs,
        out_shape=out_shape,
        compiler_params=_params("parallel"),
        name="moe_combine",
    )(*args)
    return (outs[0], outs[1]) if with_norm else (outs[0], None)


def _split_router_weights(w_group, w_expert):
    d = w_group.shape[0]
    w = jnp.concatenate([w_group.astype(F32), w_expert.astype(F32),
                         jnp.zeros((d, LANES - N_GROUPS - N_EXPERTS), F32)], axis=1)
    hi = w.astype(BF16)
    lo = (w - hi.astype(F32)).astype(BF16)
    return jnp.concatenate([hi, lo], axis=1)


def _hier_moe(h, ffn_gain, w_group, w_expert, w_gate, w_up, w_down, layer, next_gain=None):
    n, d = h.shape
    rows = EXPERT_ROWS
    xn, meta, cnt = _router(h, ffn_gain, _split_router_weights(w_group, w_expert))
    ids = jnp.arange(N_EXPERTS, dtype=jnp.int32)
    expert = meta[:, :TOP_K].astype(jnp.int32).reshape(-1)
    counts = cnt[0, N_GROUPS:N_GROUPS + N_EXPERTS].astype(jnp.int32)
    n_assign = n * TOP_K
    n_blocks = n_assign // rows + N_EXPERTS
    n_rows = n_blocks * rows
    order = jnp.argsort(expert, stable=True).astype(jnp.int32)
    start = jnp.cumsum(counts) - counts
    pad_counts = (counts + rows - 1) // rows * rows
    pad_end = jnp.cumsum(pad_counts)
    pad_start = pad_end - pad_counts
    n_used = (pad_end[-1] // rows).astype(jnp.int32).reshape(1)
    blk = jnp.minimum(jnp.arange(n_blocks, dtype=jnp.int32), n_used[0] - 1)
    block_expert = jnp.sum((pad_end[None, :] <= (blk * rows)[:, None]).astype(jnp.int32), axis=1)
    block_expert = jnp.minimum(block_expert, N_EXPERTS - 1)
    r = jnp.arange(n_rows, dtype=jnp.int32)
    row_expert = jnp.repeat(block_expert, rows)
    within = r - pad_start[row_expert]
    real = (r < n_used[0] * rows) & (within < counts[row_expert])
    choice = order[jnp.clip(start[row_expert] + within, 0, n_assign - 1)]
    spare = TOP_K * n + (r // rows % 2) * rows + r % rows
    row_tok = jnp.where(real, choice // TOP_K, 0)
    row_dst = jnp.where(real, (choice % TOP_K) * n + choice // TOP_K, spare)
    nonempty = counts > 0
    position = jnp.cumsum(nonempty.astype(jnp.int32)) - 1
    later = jnp.where(nonempty[None, :] & (ids[None, :] > ids[:, None]), ids[None, :], N_EXPERTS)
    following = jnp.min(later, axis=1)
    following = jnp.where(following < N_EXPERTS, following + layer * N_EXPERTS, -1)
    y2 = _expert_mlp(xn, row_tok, row_dst, block_expert + layer * N_EXPERTS, n_used,
                     position[block_expert] % 2, following[block_expert], w_gate, w_up, w_down,
                     TOP_K * n + 2 * rows, rows)
    return _combine(h, meta, y2, next_gain)


def kernel(x, positions, mix_norm, ffn_norm, hgrn_w_in, hgrn_out_gain, hgrn_w_out, hgrn_lb_logits, diff_w_in, diff_q_gain, diff_k_gain, diff_lambda, diff_sub_gain, diff_w_out, moe_w_group, moe_w_expert, moe_w_gate, moe_w_up, moe_w_down):
    b, s, d = x.shape
    n = b * s
    depth = mix_norm.shape[0]
    lower_bounds = jnp.cumsum(jax.nn.softmax(hgrn_lb_logits.astype(F32), axis=0), axis=0)
    h = x.reshape(n, d)
    w_gate = moe_w_gate.reshape((-1,) + moe_w_gate.shape[2:])
    w_up = moe_w_up.reshape((-1,) + moe_w_up.shape[2:])
    w_down = moe_w_down.reshape((-1,) + moe_w_down.shape[2:])
    hn = _rmsnorm_bf16(h, mix_norm[0])
    for layer in range(depth):
        j = layer // N_MIXERS
        if layer % N_MIXERS == 0:
            w_heads = (hgrn_w_in[j].reshape(d, 4, HGRN_HEADS, HGRN_DIM).transpose(2, 0, 1, 3)
                       .reshape(HGRN_HEADS, d, 4 * HGRN_DIM).astype(BF16))
            o = _hgrn_mixer(hn.reshape(b, s, d), w_heads, lower_bounds[layer], hgrn_out_gain[j])
            h = _matmul_residual(o.reshape(n, d), hgrn_w_out[j].astype(BF16), h)
        else:
            lam_init = 0.8 - 0.6 * math.exp(-0.3 * layer)
            hd = DIFF_HEAD_DIM
            half = hd // 2
            inv_freq = ROPE_THETA ** (-jnp.arange(half, dtype=F32) / half)
            ang = positions.astype(F32).reshape(n, 1) * inv_freq[None, :]
            cosf = jnp.concatenate([jnp.cos(ang), jnp.cos(ang)], axis=1)
            sins = jnp.concatenate([-jnp.sin(ang), jnp.sin(ang)], axis=1)
            n_heads = 2 * DIFF_HEADS
            n_qk = 2 * n_heads * hd
            gains = jnp.concatenate([jnp.tile(diff_q_gain[j].astype(F32), n_heads) * (hd ** -0.5),
                                     jnp.tile(diff_k_gain[j].astype(F32), n_heads)]).reshape(1, -1)
            w_in = diff_w_in[j].astype(BF16)
            qk = _qk_proj(hn, w_in[:, :n_qk], gains, cosf, sins)
            v = _matmul_bf16(hn, w_in[:, n_qk:])
            o = _diff_attention(qk.reshape(b, s, -1), v.reshape(b, s, -1), diff_lambda[j], diff_sub_gain[j],
                                lam_init)
            h = _matmul_residual(o.reshape(n, d), diff_w_out[j].astype(BF16), h)
        next_gain = mix_norm[layer + 1] if layer + 1 < depth else None
        h, hn = _hier_moe(h, ffn_norm[layer], moe_w_group[layer], moe_w_expert[layer], w_gate, w_up, w_down, layer,
                          next_gain)
    return h.reshape(b, s, d)
```

```python
import functools
import math

import jax
import jax.numpy as jnp
import numpy as np
from jax import lax
from jax.experimental import pallas as pl
from jax.experimental.pallas import tpu as pltpu

F32 = jnp.float32
BF16 = jnp.bfloat16

NORM_EPS = 1e-6
N_MIXERS = 2
HGRN_HEADS = 16
HGRN_DIM = 128
HGRN_BLOCK = 128
HGRN_CHUNK = 512
HGRN_SUB = 16
DIFF_HEADS = 8
DIFF_HEAD_DIM = 128
ROPE_THETA = 10000.0
ATTN_TILE = 256
N_GROUPS = 8
EXPERTS_PER_GROUP = 8
N_EXPERTS = N_GROUPS * EXPERTS_PER_GROUP
TOP_K = 2
EXPERT_ROWS = 256
GATHER_SLOTS = 3
RESULT_SLOTS = 3
ROW_DMA_PRIORITY = 0
WEIGHT_DMA_PRIORITY = 1
LANES = 128
VMEM_LIMIT_BYTES = 56 * 1024 * 1024


def _params(*semantics):
    return pltpu.CompilerParams(dimension_semantics=semantics, vmem_limit_bytes=VMEM_LIMIT_BYTES)


def _norm_kernel(h_ref, g_ref, o_ref):
    x = h_ref[...]
    ms = jnp.mean(x * x, axis=-1, keepdims=True)
    o_ref[...] = (x * lax.rsqrt(ms + NORM_EPS) * g_ref[...]).astype(o_ref.dtype)


def _rmsnorm_bf16(h, gain, tm=512):
    n, d = h.shape
    return pl.pallas_call(
        _norm_kernel,
        grid=(n // tm,),
        in_specs=[pl.BlockSpec((tm, d), lambda i: (i, 0)), pl.BlockSpec((1, d), lambda i: (0, 0))],
        out_specs=pl.BlockSpec((tm, d), lambda i: (i, 0)),
        out_shape=jax.ShapeDtypeStruct((n, d), BF16),
        compiler_params=_params("parallel"),
        name="rmsnorm_bf16",
    )(h, gain.reshape(1, d).astype(F32))


def _mm_res_kernel(a_ref, w_ref, r_ref, o_ref):
    o_ref[...] = r_ref[...] + jnp.dot(a_ref[...], w_ref[...], preferred_element_type=F32)


def _matmul_residual(a, w, res, tm=1024, tn=1024):
    n, k = a.shape
    nout = w.shape[1]
    tm, tn = min(tm, n), min(tn, nout)
    return pl.pallas_call(
        _mm_res_kernel,
        grid=(n // tm, nout // tn),
        in_specs=[pl.BlockSpec((tm, k), lambda i, j: (i, 0)),
                  pl.BlockSpec((k, tn), lambda i, j: (0, j)),
                  pl.BlockSpec((tm, tn), lambda i, j: (i, j))],
        out_specs=pl.BlockSpec((tm, tn), lambda i, j: (i, j)),
        out_shape=jax.ShapeDtypeStruct((n, nout), F32),
        compiler_params=_params("parallel", "parallel"),
        name="matmul_residual",
    )(a, w, res)


def _sigmoid(x):
    return 1.0 / (1.0 + jnp.exp(-x))


def _ref_rows(a_ref, half, block):
    span = 2 * half
    parts = [jnp.broadcast_to(a_ref[g * span + half - 1:g * span + half, :], (span, a_ref.shape[1]))
             for g in range(block // span)]
    return parts[0] if len(parts) == 1 else jnp.concatenate(parts, axis=0)


def _cumsum_rows(x, row):
    shift = 1
    while shift < x.shape[0]:
        x = x + jnp.where(row >= shift, pltpu.roll(x, shift, 0), 0.0)
        shift *= 2
    return x


def _hgrn_kernel(hn_ref, w_ref, lb_ref, og_ref, o_ref, proj_a, proj_b, cum_a, cum_b, *, seq, block, chunk):
    dim = HGRN_DIM
    blocks_per_chunk = chunk // block
    n_chunks = seq // chunk
    lb = lb_ref[0]
    og = og_ref[...]
    t_idx = lax.broadcasted_iota(jnp.int32, (block, block), 0)
    s_idx = lax.broadcasted_iota(jnp.int32, (block, block), 1)
    causal = s_idx <= t_idx
    split = t_idx ^ s_idx
    row = lax.broadcasted_iota(jnp.int32, (block, dim), 0)
    halves = []
    half = block // 2
    while half >= HGRN_SUB:
        halves.append(half)
        half //= 2
    contract_last = (((1,), (1,)), ((), ()))
    contract_first = (((0,), (0,)), ((), ()))

    def project(c, buf):
        r0 = pl.multiple_of(c * chunk, chunk)
        buf[...] = jnp.dot(hn_ref[0, pl.ds(r0, chunk), :], w_ref[0], preferred_element_type=F32)

    def one_block(buf, cum_ref, j, r0, state_t):
        rows = slice(j * block, (j + 1) * block)
        pq = buf[rows, 0 * dim:1 * dim]
        pf = buf[rows, 1 * dim:2 * dim]
        v16 = buf[rows, 2 * dim:3 * dim].astype(BF16)
        pg = buf[rows, 3 * dim:4 * dim]
        q = pq * _sigmoid(pq)
        forget = lb + (1.0 - lb) * _sigmoid(pf)
        k = 1.0 - forget
        cum = _cumsum_rows(jnp.log(forget), row)
        cum_ref[j] = cum
        blk = cum_ref.at[j]
        total = blk[block - 1:block, :]
        scores = None
        for half in halves:
            e = jnp.exp(-jnp.abs(cum - _ref_rows(blk, half, block)))
            p = lax.dot_general((q * e).astype(BF16), (k * e).astype(BF16), contract_last,
                                preferred_element_type=F32)
            scores = p if scores is None else jnp.where(split >= 2 * half, scores, p)
        sub_ref = jnp.concatenate(
            [jnp.zeros((HGRN_SUB, dim), F32)]
            + [jnp.broadcast_to(blk[s * HGRN_SUB - 1:s * HGRN_SUB, :], (HGRN_SUB, dim))
               for s in range(1, block // HGRN_SUB)], axis=0)
        loc = cum - sub_ref
        p = lax.dot_general((q * jnp.exp(loc)).astype(BF16), (k * jnp.exp(-loc)).astype(BF16),
                            contract_last, preferred_element_type=F32)
        scores = p if scores is None else jnp.where(split >= HGRN_SUB, scores, p)
        scores = jnp.where(causal, scores, 0.0)
        o = jnp.dot(scores.astype(BF16), v16, preferred_element_type=F32)
        o = o + lax.dot_general((q * jnp.exp(cum)).astype(BF16), state_t.astype(BF16), contract_last,
                                preferred_element_type=F32)
        k_end = (k * jnp.exp(total - cum)).astype(BF16)
        state_t = state_t * jnp.exp(total) + lax.dot_general(v16, k_end, contract_first,
                                                              preferred_element_type=F32)
        ms = jnp.mean(o * o, axis=-1, keepdims=True)
        y = o * lax.rsqrt(ms + NORM_EPS) * og
        o_ref[0, pl.ds(r0 + j * block, block), :] = (y * (pg * _sigmoid(pg))).astype(o_ref.dtype)
        return state_t

    bufs = (proj_a, proj_b)
    cum_refs = (cum_a, cum_b)
    project(0, bufs[0])

    def chunk_pair(cp, state_t):
        for parity in range(2):
            c = 2 * cp + parity
            project(jnp.minimum(c + 1, n_chunks - 1), bufs[1 - parity])
            r0 = pl.multiple_of(c * chunk, chunk)
            for j in range(blocks_per_chunk):
                state_t = one_block(bufs[parity], cum_refs[parity], j, r0, state_t)
        return state_t

    state = lax.fori_loop(0, n_chunks // 2, chunk_pair, jnp.zeros((dim, dim), F32))
    if n_chunks % 2:
        r0 = (n_chunks - 1) * chunk
        for j in range(blocks_per_chunk):
            state = one_block(bufs[0], cum_refs[0], j, r0, state)


def _hgrn_mixer(hn, w_heads, lb, out_gain):
    b, s, d = hn.shape
    dim = HGRN_DIM
    block = min(HGRN_BLOCK, s)
    chunk = min(HGRN_CHUNK, s)
    return pl.pallas_call(
        functools.partial(_hgrn_kernel, seq=s, block=block, chunk=chunk),
        grid=(b, HGRN_HEADS),
        in_specs=[pl.BlockSpec((1, s, d), lambda i, h: (i, 0, 0)),
                  pl.BlockSpec((1, d, 4 * dim), lambda i, h: (h, 0, 0)),
                  pl.BlockSpec((1, 1, dim), lambda i, h: (h, 0, 0)),
                  pl.BlockSpec((1, dim), lambda i, h: (0, 0))],
        out_specs=pl.BlockSpec((1, s, dim), lambda i, h: (i, 0, h)),
        out_shape=jax.ShapeDtypeStruct((b, s, d), BF16),
        scratch_shapes=[pltpu.VMEM((chunk, 4 * dim), F32), pltpu.VMEM((chunk, 4 * dim), F32),
                        pltpu.VMEM((chunk // block, block, dim), F32),
                        pltpu.VMEM((chunk // block, block, dim), F32)],
        compiler_params=_params("parallel", "arbitrary"),
        name="hgrn_mixer",
    )(hn, w_heads, lb.reshape(HGRN_HEADS, 1, dim), out_gain.reshape(1, dim).astype(F32))


def _qk_kernel(a_ref, w_ref, g_ref, cos_ref, sin_ref, o_ref, *, tn):
    hd = DIFF_HEAD_DIM
    a = a_ref[...]
    cosf = cos_ref[...]
    sins = sin_ref[...]
    for c0 in range(0, tn, 2 * hd):
        acc = jnp.dot(a, w_ref[:, c0:c0 + 2 * hd], preferred_element_type=F32)
        for hh in range(2):
            x = acc[:, hh * hd:(hh + 1) * hd]
            sl = slice(c0 + hh * hd, c0 + (hh + 1) * hd)
            ms = jnp.mean(x * x, axis=-1, keepdims=True)
            y = x * lax.rsqrt(ms + NORM_EPS) * g_ref[:, sl]
            rot = pltpu.roll(y, hd // 2, 1)
            o_ref[:, sl] = (y * cosf + rot * sins).astype(o_ref.dtype)


def _qk_proj(hn, w, gains, cosf, sins, tm=512, tn=1024):
    n, d = hn.shape
    nout = w.shape[1]
    tm = min(tm, n)
    return pl.pallas_call(
        functools.partial(_qk_kernel, tn=tn),
        grid=(n // tm, nout // tn),
        in_specs=[pl.BlockSpec((tm, d), lambda i, j: (i, 0)),
                  pl.BlockSpec((d, tn), lambda i, j: (0, j)),
                  pl.BlockSpec((1, tn), lambda i, j: (0, j)),
                  pl.BlockSpec((tm, DIFF_HEAD_DIM), lambda i, j: (i, 0)),
                  pl.BlockSpec((tm, DIFF_HEAD_DIM), lambda i, j: (i, 0))],
        out_specs=pl.BlockSpec((tm, tn), lambda i, j: (i, j)),
        out_shape=jax.ShapeDtypeStruct((n, nout), BF16),
        compiler_params=_params("parallel", "arbitrary"),
        name="qk_proj",
    )(hn, w, gains, cosf, sins)


def _mm_bf16_kernel(a_ref, w_ref, o_ref):
    o_ref[...] = jnp.dot(a_ref[...], w_ref[...], preferred_element_type=F32).astype(o_ref.dtype)


def _matmul_bf16(a, w, tm=1024, tn=1024):
    n, k = a.shape
    nout = w.shape[1]
    tm, tn = min(tm, n), min(tn, nout)
    return pl.pallas_call(
        _mm_bf16_kernel,
        grid=(n // tm, nout // tn),
        in_specs=[pl.BlockSpec((tm, k), lambda i, j: (i, 0)), pl.BlockSpec((k, tn), lambda i, j: (0, j))],
        out_specs=pl.BlockSpec((tm, tn), lambda i, j: (i, j)),
        out_shape=jax.ShapeDtypeStruct((n, nout), BF16),
        compiler_params=_params("parallel", "parallel"),
        name="matmul_bf16",
    )(a, w)


def _attn_kernel(q_ref, k_ref, v_ref, lam_ref, sg_ref, o_ref, s_ref, m_ref, l_ref, acc_ref, *, seq, tile, lam_init):
    hd = DIFF_HEAD_DIM
    lf = lam_ref[...]
    lam = (jnp.exp(jnp.sum(lf[0:1] * lf[1:2], axis=-1, keepdims=True))
           - jnp.exp(jnp.sum(lf[2:3] * lf[3:4], axis=-1, keepdims=True)) + lam_init)
    row = lax.broadcasted_iota(jnp.int32, (tile, tile), 0)
    col = lax.broadcasted_iota(jnp.int32, (tile, tile), 1)
    contract_last = (((1,), (1,)), ((), ()))

    def lane_fold(x, op):
        out = x[:, :LANES]
        for t in range(1, tile // LANES):
            out = op(out, x[:, t * LANES:(t + 1) * LANES])
        return out

    def qblock(i, carry):
        q0 = pl.multiple_of(i * tile, tile)

        def score_tile(jb, c2):
            k0 = pl.multiple_of(jb * tile, tile)
            for c in range(2):
                q = q_ref[0, pl.ds(q0, tile), c * hd:(c + 1) * hd]
                kj = k_ref[0, pl.ds(k0, tile), c * hd:(c + 1) * hd]
                s = lax.dot_general(q, kj, contract_last, preferred_element_type=F32)
                s_ref[c, :, pl.ds(k0, tile)] = s
                m_ref[c] = jnp.maximum(m_ref[c], lane_fold(s, jnp.maximum))
            return c2

        for c in range(2):
            m_ref[c] = jnp.full((tile, LANES), -jnp.inf, F32)
        lax.fori_loop(0, i, score_tile, 0)
        for c in range(2):
            q = q_ref[0, pl.ds(q0, tile), c * hd:(c + 1) * hd]
            kd = k_ref[0, pl.ds(q0, tile), c * hd:(c + 1) * hd]
            s = lax.dot_general(q, kd, contract_last, preferred_element_type=F32)
            s = jnp.where(col <= row, s, -jnp.inf)
            s_ref[c, :, pl.ds(q0, tile)] = s
            m_part = jnp.maximum(m_ref[c], lane_fold(s, jnp.maximum))
            m_ref[c] = jnp.broadcast_to(jnp.max(m_part, axis=-1, keepdims=True), (tile, LANES))
            l_ref[c] = jnp.zeros((tile, LANES), F32)
            acc_ref[c] = jnp.zeros((tile, 2 * hd), F32)

        def value_tile(jb, c2):
            k0 = pl.multiple_of(jb * tile, tile)
            vj = v_ref[0, pl.ds(k0, tile), :]
            for c in range(2):
                m = m_ref[c]
                e = jnp.exp(s_ref[c, :, pl.ds(k0, tile)] - jnp.concatenate([m] * (tile // LANES), axis=1))
                l_ref[c] = l_ref[c] + lane_fold(e, jnp.add)
                acc_ref[c] = acc_ref[c] + jnp.dot(e.astype(BF16), vj, preferred_element_type=F32)
            return c2

        lax.fori_loop(0, i + 1, value_tile, 0)
        inv = [1.0 / jnp.sum(l_ref[c], axis=-1, keepdims=True) for c in range(2)]
        o = acc_ref[0] * inv[0] - lam * (acc_ref[1] * inv[1])
        ms = jnp.mean(o * o, axis=-1, keepdims=True)
        y = o * lax.rsqrt(ms + NORM_EPS) * sg_ref[...]
        o_ref[0, pl.ds(q0, tile), :] = (y * (1.0 - lam_init)).astype(o_ref.dtype)
        return carry

    lax.fori_loop(0, seq // tile, qblock, 0)


def _diff_attention(qk, v, lam, sub_gain, lam_init):
    b, s, _ = qk.shape
    hd = DIFF_HEAD_DIM
    tile = min(ATTN_TILE, s)
    nh = DIFF_HEADS
    return pl.pallas_call(
        functools.partial(_attn_kernel, seq=s, tile=tile, lam_init=lam_init),
        grid=(b, nh),
        in_specs=[pl.BlockSpec((1, s, 2 * hd), lambda i, h: (i, 0, h)),
                  pl.BlockSpec((1, s, 2 * hd), lambda i, h: (i, 0, nh + h)),
                  pl.BlockSpec((1, s, 2 * hd), lambda i, h: (i, 0, h)),
                  pl.BlockSpec((4, hd), lambda i, h: (0, 0)),
                  pl.BlockSpec((1, 2 * hd), lambda i, h: (0, 0))],
        out_specs=pl.BlockSpec((1, s, 2 * hd), lambda i, h: (i, 0, h)),
        out_shape=jax.ShapeDtypeStruct((b, s, nh * 2 * hd), BF16),
        scratch_shapes=[pltpu.VMEM((2, tile, s), F32), pltpu.VMEM((2, tile, LANES), F32),
                        pltpu.VMEM((2, tile, LANES), F32), pltpu.VMEM((2, tile, 2 * hd), F32)],
        compiler_params=_params("parallel", "parallel"),
        name="diff_attention",
    )(qk, qk, v, lam.astype(F32), sub_gain.reshape(1, 2 * hd).astype(F32))


def _router_kernel(h_ref, g_ref, w_ref, xn_ref, meta_ref, cnt_ref):
    x = h_ref[...]
    ms = jnp.mean(x * x, axis=-1, keepdims=True)
    xn = x * lax.rsqrt(ms + NORM_EPS) * g_ref[...]
    xn_ref[...] = xn
    x_hi = xn.astype(BF16)
    x_lo = (xn - x_hi.astype(F32)).astype(BF16)
    both = jnp.dot(x_hi, w_ref[...], preferred_element_type=F32)
    lg = both[:, :LANES] + both[:, LANES:] + jnp.dot(x_lo, w_ref[:, :LANES], preferred_element_type=F32)
    lane = lax.broadcasted_iota(jnp.int32, lg.shape, 1)
    is_group = lane < N_GROUPS
    gmax = jnp.max(jnp.where(is_group, lg, -jnp.inf), axis=-1, keepdims=True)
    ge = jnp.where(is_group, jnp.exp(lg - gmax), 0.0)
    gp = ge / jnp.sum(ge, axis=-1, keepdims=True)
    g_top_p = jnp.max(jnp.where(is_group, gp, -1.0), axis=-1, keepdims=True)
    g_top = jnp.min(jnp.where(is_group & (gp == g_top_p), lane, LANES), axis=-1, keepdims=True)
    first = N_GROUPS + g_top * EXPERTS_PER_GROUP
    in_group = (lane >= first) & (lane < first + EXPERTS_PER_GROUP)
    emax = jnp.max(jnp.where(in_group, lg, -jnp.inf), axis=-1, keepdims=True)
    ee = jnp.where(in_group, jnp.exp(lg - emax), 0.0)
    ep = ee / jnp.sum(ee, axis=-1, keepdims=True)
    p1 = jnp.max(jnp.where(in_group, ep, -1.0), axis=-1, keepdims=True)
    i1 = jnp.min(jnp.where(in_group & (ep == p1), lane, LANES), axis=-1, keepdims=True)
    rest = in_group & (lane != i1)
    p2 = jnp.max(jnp.where(rest, ep, -1.0), axis=-1, keepdims=True)
    i2 = jnp.min(jnp.where(rest & (ep == p2), lane, LANES), axis=-1, keepdims=True)
    denom = p1 + p2
    gate1 = g_top_p * p1 / denom
    gate2 = g_top_p * p2 / denom
    e1 = (i1 - N_GROUPS).astype(F32)
    e2 = (i2 - N_GROUPS).astype(F32)

    @pl.when(pl.program_id(0) == 0)
    def _():
        cnt_ref[...] = jnp.zeros_like(cnt_ref)

    chosen = jnp.where((lane == i1) | (lane == i2), 1.0, 0.0)
    cnt_ref[...] = cnt_ref[...] + jnp.sum(chosen, axis=0, keepdims=True)
    meta_ref[...] = jnp.where(lane == 0, e1, jnp.where(lane == 1, e2, jnp.where(lane == 2, gate1,
                              jnp.where(lane == 3, gate2, 0.0))))


def _router(h, gain, w_split, tm=256):
    n, d = h.shape
    tm = min(tm, n)
    return pl.pallas_call(
        _router_kernel,
        grid=(n // tm,),
        in_specs=[pl.BlockSpec((tm, d), lambda i: (i, 0)),
                  pl.BlockSpec((1, d), lambda i: (0, 0)),
                  pl.BlockSpec((d, 2 * LANES), lambda i: (0, 0))],
        out_specs=[pl.BlockSpec((tm, d), lambda i: (i, 0)), pl.BlockSpec((tm, LANES), lambda i: (i, 0)),
                   pl.BlockSpec((1, LANES), lambda i: (0, 0))],
        out_shape=[jax.ShapeDtypeStruct((n, d), F32), jax.ShapeDtypeStruct((n, LANES), F32),
                   jax.ShapeDtypeStruct((1, LANES), F32)],
        compiler_params=_params("arbitrary"),
        name="moe_router",
    )(h, gain.reshape(1, d).astype(F32), w_split)


def _row_copy(src_ref, dst_ref, src_row, dst_row, sem):
    return pltpu.make_async_copy(src_ref.at[pl.ds(src_row, 1)], dst_ref.at[pl.ds(dst_row, 1)], sem)


def _expert_kernel(be_ref, nb_ref, tok_ref, dst_ref, par_ref, nxt_ref,
                   xn_hbm, wg_hbm, wu_hbm, wd_hbm, y_hbm,
                   xbuf, ybuf, xsem, ysem, wg32, wu32, wd32, wsem, wg16, wu16, wd16, *, rows):
    i = pl.program_id(0)
    n_used = nb_ref[0]
    slot = lax.rem(i, RESULT_SLOTS)
    prev = lax.rem(i + RESULT_SLOTS - 1, RESULT_SLOTS)
    xslot = lax.rem(i, GATHER_SLOTS)
    stages = ((wg_hbm, wg32), (wu_hbm, wu32), (wd_hbm, wd32))

    def weights(expert, s):
        return [pltpu.make_async_copy(hbm.at[expert], stage.at[s], wsem.at[s]) for hbm, stage in stages]

    def gather(block, into):
        for r in range(rows):
            _row_copy(xn_hbm, xbuf.at[into], tok_ref[block * rows + r], r, xsem.at[into]).start(ROW_DMA_PRIORITY)

    def scatter(block, frm):
        for r in range(rows):
            _row_copy(ybuf.at[frm], y_hbm, r, dst_ref[block * rows + r], ysem.at[frm]).start(ROW_DMA_PRIORITY)

    def wait_gather(s):
        pltpu.make_async_copy(xn_hbm.at[pl.ds(0, rows)], xbuf.at[s], xsem.at[s]).wait()

    def wait_scatter(s):
        pltpu.make_async_copy(ybuf.at[s], y_hbm.at[pl.ds(0, rows)], ysem.at[s]).wait()

    def ahead(k):
        return jnp.minimum(i + k, n_used - 1)

    def mlp():
        x = xbuf[xslot].astype(BF16)
        g = jnp.dot(x, wg16[...], preferred_element_type=F32)
        u = jnp.dot(x, wu16[...], preferred_element_type=F32)
        hmid = (g * _sigmoid(g)) * u
        ybuf[slot] = jnp.dot(hmid.astype(BF16), wd16[...], preferred_element_type=F32)

    @pl.when(i == 0)
    def _():
        for c in weights(be_ref[0], 0):
            c.start(WEIGHT_DMA_PRIORITY)
        for k in range(GATHER_SLOTS - 1):
            gather(ahead(k), k)
        ybuf[1] = jnp.zeros((rows, ybuf.shape[2]), F32)
        n_real = y_hbm.shape[0] - RESULT_SLOTS * rows
        clears = [pltpu.make_async_copy(ybuf.at[1], y_hbm.at[pl.ds(n_real + p * rows, rows)], ysem.at[1])
                  for p in range(RESULT_SLOTS)]
        for c in clears:
            c.start()
        for c in clears:
            c.wait()

    new_expert = jnp.logical_or(i == 0, be_ref[i] != be_ref[jnp.maximum(i - 1, 0)])

    @pl.when(jnp.logical_and(i < n_used, new_expert))
    def _():
        s = par_ref[i]
        for c in weights(0, s):
            c.wait()

        @pl.when(nxt_ref[i] >= 0)
        def _():
            for c in weights(nxt_ref[i], 1 - s):
                c.start(WEIGHT_DMA_PRIORITY)

        wg16[...] = wg32[s].astype(BF16)
        wu16[...] = wu32[s].astype(BF16)
        wd16[...] = wd32[s].astype(BF16)

    @pl.when(i == 0)
    def _():
        wait_gather(xslot)
        gather(ahead(GATHER_SLOTS - 1), lax.rem(i + GATHER_SLOTS - 1, GATHER_SLOTS))
        mlp()

    @pl.when(jnp.logical_and(i >= 1, i < n_used))
    def _():
        @pl.when(i >= RESULT_SLOTS)
        def _():
            wait_scatter(slot)

        wait_gather(xslot)
        gather(ahead(GATHER_SLOTS - 1), lax.rem(i + GATHER_SLOTS - 1, GATHER_SLOTS))
        scatter(i - 1, prev)
        mlp()

    @pl.when(i == n_used)
    def _():
        scatter(i - 1, prev)
        for k in range(1, RESULT_SLOTS + 1):
            @pl.when(i >= k)
            def _():
                wait_scatter(lax.rem(i + RESULT_SLOTS - k, RESULT_SLOTS))

        for k in range(GATHER_SLOTS - 1):
            wait_gather(lax.rem(i + k, GATHER_SLOTS))


def _expert_mlp(xn, row_tok, row_dst, block_expert, n_used, parity, next_expert, w_gate, w_up, w_down, n_out,
                rows=EXPERT_ROWS):
    d = xn.shape[1]
    n_rows = row_tok.shape[0]
    ff = w_gate.shape[-1]
    any_spec = pl.BlockSpec(memory_space=pl.ANY)
    return pl.pallas_call(
        functools.partial(_expert_kernel, rows=rows),
        grid_spec=pltpu.PrefetchScalarGridSpec(
            num_scalar_prefetch=6,
            grid=(n_rows // rows,),
            in_specs=[any_spec, any_spec, any_spec, any_spec],
            out_specs=any_spec,
            scratch_shapes=[pltpu.VMEM((GATHER_SLOTS, rows, d), F32), pltpu.VMEM((RESULT_SLOTS, rows, d), F32),
                            pltpu.SemaphoreType.DMA((GATHER_SLOTS,)), pltpu.SemaphoreType.DMA((RESULT_SLOTS,)),
                            pltpu.VMEM((2, d, ff), F32), pltpu.VMEM((2, d, ff), F32), pltpu.VMEM((2, ff, d), F32),
                            pltpu.SemaphoreType.DMA((2,)),
                            pltpu.VMEM((d, ff), BF16), pltpu.VMEM((d, ff), BF16), pltpu.VMEM((ff, d), BF16)]),
        out_shape=jax.ShapeDtypeStruct((n_out, d), F32),
        compiler_params=_params("arbitrary"),
        name="moe_experts",
    )(block_expert, n_used, row_tok, row_dst, parity, next_expert, xn, w_gate, w_up, w_down)


def _combine_kernel(*refs, with_norm):
    if with_norm:
        h_ref, meta_ref, y0_ref, y1_ref, g_ref, o_ref, hn_ref = refs
    else:
        h_ref, meta_ref, y0_ref, y1_ref, o_ref = refs
    meta = meta_ref[...]
    out = h_ref[...] + (y0_ref[...] * meta[:, 2:3] + y1_ref[...] * meta[:, 3:4])
    o_ref[...] = out
    if with_norm:
        ms = jnp.mean(out * out, axis=-1, keepdims=True)
        hn_ref[...] = (out * lax.rsqrt(ms + NORM_EPS) * g_ref[...]).astype(hn_ref.dtype)


def _combine(h, meta, y2, next_gain=None, tm=256):
    n, d = h.shape
    tm = min(tm, n)
    with_norm = next_gain is not None
    row_spec = pl.BlockSpec((tm, d), lambda i: (i, 0))
    in_specs = [row_spec, pl.BlockSpec((tm, LANES), lambda i: (i, 0)), row_spec,
                pl.BlockSpec((tm, d), lambda i: (n // tm + i, 0))]
    args = [h, meta, y2, y2]
    out_specs, out_shape = [row_spec], [jax.ShapeDtypeStruct((n, d), F32)]
    if with_norm:
        in_specs.append(pl.BlockSpec((1, d), lambda i: (0, 0)))
        args.append(next_gain.reshape(1, d).astype(F32))
        out_specs.append(row_spec)
        out_shape.append(jax.ShapeDtypeStruct((n, d), BF16))
    outs = pl.pallas_call(
        functools.partial(_combine_kernel, with_norm=with_norm),
        grid=(n // tm,),
        in_specs=in_specs,
        out_specs=out_specs,
        out_shape=out_shape,
        compiler_params=_params("parallel"),
        name="moe_combine",
    )(*args)
    return (outs[0], outs[1]) if with_norm else (outs[0], None)


def _split_router_weights(w_group, w_expert):
    d = w_group.shape[0]
    w = jnp.concatenate([w_group.astype(F32), w_expert.astype(F32),
                         jnp.zeros((d, LANES - N_GROUPS - N_EXPERTS), F32)], axis=1)
    hi = w.astype(BF16)
    lo = (w - hi.astype(F32)).astype(BF16)
    return jnp.concatenate([hi, lo], axis=1)


def _hier_moe(h, ffn_gain, w_group, w_expert, w_gate, w_up, w_down, layer, next_gain=None):
    n, d = h.shape
    rows = EXPERT_ROWS
    xn, meta, cnt = _router(h, ffn_gain, _split_router_weights(w_group, w_expert))
    ids = jnp.arange(N_EXPERTS, dtype=jnp.int32)
    expert = meta[:, :TOP_K].astype(jnp.int32).reshape(-1)
    counts = cnt[0, N_GROUPS:N_GROUPS + N_EXPERTS].astype(jnp.int32)
    n_assign = n * TOP_K
    n_blocks = n_assign // rows + N_EXPERTS
    n_rows = n_blocks * rows
    order = jnp.argsort(expert, stable=True).astype(jnp.int32)
    start = jnp.cumsum(counts) - counts
    pad_counts = (counts + rows - 1) // rows * rows
    pad_end = jnp.cumsum(pad_counts)
    pad_start = pad_end - pad_counts
    n_used = (pad_end[-1] // rows).astype(jnp.int32).reshape(1)
    blk = jnp.minimum(jnp.arange(n_blocks, dtype=jnp.int32), n_used[0] - 1)
    block_expert = jnp.sum((pad_end[None, :] <= (blk * rows)[:, None]).astype(jnp.int32), axis=1)
    block_expert = jnp.minimum(block_expert, N_EXPERTS - 1)
    r = jnp.arange(n_rows, dtype=jnp.int32)
    row_expert = jnp.repeat(block_expert, rows)
    within = r - pad_start[row_expert]
    real = (r < n_used[0] * rows) & (within < counts[row_expert])
    choice = order[jnp.clip(start[row_expert] + within, 0, n_assign - 1)]
    spare = TOP_K * n + (r // rows % RESULT_SLOTS) * rows + r % rows
    row_tok = jnp.where(real, choice // TOP_K, 0)
    row_dst = jnp.where(real, (choice % TOP_K) * n + choice // TOP_K, spare)
    nonempty = counts > 0
    position = jnp.cumsum(nonempty.astype(jnp.int32)) - 1
    later = jnp.where(nonempty[None, :] & (ids[None, :] > ids[:, None]), ids[None, :], N_EXPERTS)
    following = jnp.min(later, axis=1)
    following = jnp.where(following < N_EXPERTS, following + layer * N_EXPERTS, -1)
    y2 = _expert_mlp(xn, row_tok, row_dst, block_expert + layer * N_EXPERTS, n_used,
                     position[block_expert] % 2, following[block_expert], w_gate, w_up, w_down,
                     TOP_K * n + RESULT_SLOTS * rows, rows)
    return _combine(h, meta, y2, next_gain)


def kernel(x, positions, mix_norm, ffn_norm, hgrn_w_in, hgrn_out_gain, hgrn_w_out, hgrn_lb_logits, diff_w_in, diff_q_gain, diff_k_gain, diff_lambda, diff_sub_gain, diff_w_out, moe_w_group, moe_w_expert, moe_w_gate, moe_w_up, moe_w_down):
    b, s, d = x.shape
    n = b * s
    depth = mix_norm.shape[0]
    lower_bounds = jnp.cumsum(jax.nn.softmax(hgrn_lb_logits.astype(F32), axis=0), axis=0)
    h = x.reshape(n, d)
    w_gate = moe_w_gate.reshape((-1,) + moe_w_gate.shape[2:])
    w_up = moe_w_up.reshape((-1,) + moe_w_up.shape[2:])
    w_down = moe_w_down.reshape((-1,) + moe_w_down.shape[2:])
    hn = _rmsnorm_bf16(h, mix_norm[0])
    for layer in range(depth):
        j = layer // N_MIXERS
        if layer % N_MIXERS == 0:
            w_heads = (hgrn_w_in[j].reshape(d, 4, HGRN_HEADS, HGRN_DIM).transpose(2, 0, 1, 3)
                       .reshape(HGRN_HEADS, d, 4 * HGRN_DIM).astype(BF16))
            o = _hgrn_mixer(hn.reshape(b, s, d), w_heads, lower_bounds[layer], hgrn_out_gain[j])
            h = _matmul_residual(o.reshape(n, d), hgrn_w_out[j].astype(BF16), h)
        else:
            lam_init = 0.8 - 0.6 * math.exp(-0.3 * layer)
            hd = DIFF_HEAD_DIM
            half = hd // 2
            inv_freq = ROPE_THETA ** (-jnp.arange(half, dtype=F32) / half)
            ang = positions.astype(F32).reshape(n, 1) * inv_freq[None, :]
            cosf = jnp.concatenate([jnp.cos(ang), jnp.cos(ang)], axis=1)
            sins = jnp.concatenate([-jnp.sin(ang), jnp.sin(ang)], axis=1)
            n_heads = 2 * DIFF_HEADS
            n_qk = 2 * n_heads * hd
            gains = jnp.concatenate([jnp.tile(diff_q_gain[j].astype(F32), n_heads) * (hd ** -0.5),
                                     jnp.tile(diff_k_gain[j].astype(F32), n_heads)]).reshape(1, -1)
            w_in = diff_w_in[j].astype(BF16)
            qk = _qk_proj(hn, w_in[:, :n_qk], gains, cosf, sins)
            v = _matmul_bf16(hn, w_in[:, n_qk:])
            o = _diff_attention(qk.reshape(b, s, -1), v.reshape(b, s, -1), diff_lambda[j], diff_sub_gain[j],
                                lam_init)
            h = _matmul_residual(o.reshape(n, d), diff_w_out[j].astype(BF16), h)
        next_gain = mix_norm[layer + 1] if layer + 1 < depth else None
        h, hn = _hier_moe(h, ffn_norm[layer], moe_w_group[layer], moe_w_expert[layer], w_gate, w_up, w_down, layer,
                          next_gain)
    return h.reshape(b, s, d)
```

```python
import functools
import math

import jax
import jax.numpy as jnp
import numpy as np
from jax import lax
from jax.experimental import pallas as pl
from jax.experimental.pallas import tpu as pltpu

F32 = jnp.float32
BF16 = jnp.bfloat16

NORM_EPS = 1e-6
N_MIXERS = 2
HGRN_HEADS = 16
HGRN_DIM = 128
HGRN_BLOCK = 128
HGRN_CHUNK = 512
HGRN_SUB = 16
DIFF_HEADS = 8
DIFF_HEAD_DIM = 128
ROPE_THETA = 10000.0
ATTN_TILE = 256
N_GROUPS = 8
EXPERTS_PER_GROUP = 8
N_EXPERTS = N_GROUPS * EXPERTS_PER_GROUP
TOP_K = 2
EXPERT_ROWS = 256
GATHER_SLOTS = 3
RESULT_SLOTS = 3
ROW_DMA_PRIORITY = 0
WEIGHT_DMA_PRIORITY = 1
LANES = 128
VMEM_LIMIT_BYTES = 56 * 1024 * 1024


def _params(*semantics):
    return pltpu.CompilerParams(dimension_semantics=semantics, vmem_limit_bytes=VMEM_LIMIT_BYTES)


def _norm_kernel(h_ref, g_ref, o_ref):
    x = h_ref[...]
    ms = jnp.mean(x * x, axis=-1, keepdims=True)
    o_ref[...] = (x * lax.rsqrt(ms + NORM_EPS) * g_ref[...]).astype(o_ref.dtype)


def _rmsnorm_bf16(h, gain, tm=512):
    n, d = h.shape
    return pl.pallas_call(
        _norm_kernel,
        grid=(n // tm,),
        in_specs=[pl.BlockSpec((tm, d), lambda i: (i, 0)), pl.BlockSpec((1, d), lambda i: (0, 0))],
        out_specs=pl.BlockSpec((tm, d), lambda i: (i, 0)),
        out_shape=jax.ShapeDtypeStruct((n, d), BF16),
        compiler_params=_params("parallel"),
        name="rmsnorm_bf16",
    )(h, gain.reshape(1, d).astype(F32))


def _mm_res_kernel(a_ref, w_ref, r_ref, o_ref):
    o_ref[...] = r_ref[...] + jnp.dot(a_ref[...], w_ref[...], preferred_element_type=F32)


def _matmul_residual(a, w, res, tm=1024, tn=1024):
    n, k = a.shape
    nout = w.shape[1]
    tm, tn = min(tm, n), min(tn, nout)
    return pl.pallas_call(
        _mm_res_kernel,
        grid=(n // tm, nout // tn),
        in_specs=[pl.BlockSpec((tm, k), lambda i, j: (i, 0)),
                  pl.BlockSpec((k, tn), lambda i, j: (0, j)),
                  pl.BlockSpec((tm, tn), lambda i, j: (i, j))],
        out_specs=pl.BlockSpec((tm, tn), lambda i, j: (i, j)),
        out_shape=jax.ShapeDtypeStruct((n, nout), F32),
        compiler_params=_params("parallel", "parallel"),
        name="matmul_residual",
    )(a, w, res)


def _sigmoid(x):
    return 1.0 / (1.0 + jnp.exp(-x))


def _ref_rows(a_ref, half, block):
    span = 2 * half
    parts = [jnp.broadcast_to(a_ref[g * span + half - 1:g * span + half, :], (span, a_ref.shape[1]))
             for g in range(block // span)]
    return parts[0] if len(parts) == 1 else jnp.concatenate(parts, axis=0)


def _cumsum_rows(x, row):
    shift = 1
    while shift < x.shape[0]:
        x = x + jnp.where(row >= shift, pltpu.roll(x, shift, 0), 0.0)
        shift *= 2
    return x


def _hgrn_kernel(hn_ref, w_ref, lb_ref, og_ref, o_ref, proj_a, proj_b, cum_a, cum_b, *, seq, block, chunk):
    dim = HGRN_DIM
    blocks_per_chunk = chunk // block
    n_chunks = seq // chunk
    lb = lb_ref[0]
    og = og_ref[...]
    t_idx = lax.broadcasted_iota(jnp.int32, (block, block), 0)
    s_idx = lax.broadcasted_iota(jnp.int32, (block, block), 1)
    causal = s_idx <= t_idx
    split = t_idx ^ s_idx
    row = lax.broadcasted_iota(jnp.int32, (block, dim), 0)
    halves = []
    half = block // 2
    while half >= HGRN_SUB:
        halves.append(half)
        half //= 2
    contract_last = (((1,), (1,)), ((), ()))
    contract_first = (((0,), (0,)), ((), ()))

    def project(c, buf):
        r0 = pl.multiple_of(c * chunk, chunk)
        buf[...] = jnp.dot(hn_ref[0, pl.ds(r0, chunk), :], w_ref[0], preferred_element_type=F32)

    def one_block(buf, cum_ref, j, r0, state_t):
        rows = slice(j * block, (j + 1) * block)
        pq = buf[rows, 0 * dim:1 * dim]
        pf = buf[rows, 1 * dim:2 * dim]
        v16 = buf[rows, 2 * dim:3 * dim].astype(BF16)
        pg = buf[rows, 3 * dim:4 * dim]
        q = pq * _sigmoid(pq)
        forget = lb + (1.0 - lb) * _sigmoid(pf)
        k = 1.0 - forget
        cum = _cumsum_rows(jnp.log(forget), row)
        cum_ref[j] = cum
        blk = cum_ref.at[j]
        total = blk[block - 1:block, :]
        scores = None
        for half in halves:
            e = jnp.exp(-jnp.abs(cum - _ref_rows(blk, half, block)))
            p = lax.dot_general((q * e).astype(BF16), (k * e).astype(BF16), contract_last,
                                preferred_element_type=F32)
            scores = p if scores is None else jnp.where(split >= 2 * half, scores, p)
        sub_ref = jnp.concatenate(
            [jnp.zeros((HGRN_SUB, dim), F32)]
            + [jnp.broadcast_to(blk[s * HGRN_SUB - 1:s * HGRN_SUB, :], (HGRN_SUB, dim))
               for s in range(1, block // HGRN_SUB)], axis=0)
        loc = cum - sub_ref
        p = lax.dot_general((q * jnp.exp(loc)).astype(BF16), (k * jnp.exp(-loc)).astype(BF16),
                            contract_last, preferred_element_type=F32)
        scores = p if scores is None else jnp.where(split >= HGRN_SUB, scores, p)
        scores = jnp.where(causal, scores, 0.0)
        o = jnp.dot(scores.astype(BF16), v16, preferred_element_type=F32)
        o = o + lax.dot_general((q * jnp.exp(cum)).astype(BF16), state_t.astype(BF16), contract_last,
                                preferred_element_type=F32)
        k_end = (k * jnp.exp(total - cum)).astype(BF16)
        state_t = state_t * jnp.exp(total) + lax.dot_general(v16, k_end, contract_first,
                                                              preferred_element_type=F32)
        ms = jnp.mean(o * o, axis=-1, keepdims=True)
        y = o * lax.rsqrt(ms + NORM_EPS) * og
        o_ref[0, pl.ds(r0 + j * block, block), :] = (y * (pg * _sigmoid(pg))).astype(o_ref.dtype)
        return state_t

    bufs = (proj_a, proj_b)
    cum_refs = (cum_a, cum_b)
    project(0, bufs[0])

    def chunk_pair(cp, state_t):
        for parity in range(2):
            c = 2 * cp + parity
            project(jnp.minimum(c + 1, n_chunks - 1), bufs[1 - parity])
            r0 = pl.multiple_of(c * chunk, chunk)
            for j in range(blocks_per_chunk):
                state_t = one_block(bufs[parity], cum_refs[parity], j, r0, state_t)
        return state_t

    state = lax.fori_loop(0, n_chunks // 2, chunk_pair, jnp.zeros((dim, dim), F32))
    if n_chunks % 2:
        r0 = (n_chunks - 1) * chunk
        for j in range(blocks_per_chunk):
            state = one_block(bufs[0], cum_refs[0], j, r0, state)


def _hgrn_mixer(hn, w_heads, lb, out_gain):
    b, s, d = hn.shape
    dim = HGRN_DIM
    block = min(HGRN_BLOCK, s)
    chunk = min(HGRN_CHUNK, s)
    return pl.pallas_call(
        functools.partial(_hgrn_kernel, seq=s, block=block, chunk=chunk),
        grid=(b, HGRN_HEADS),
        in_specs=[pl.BlockSpec((1, s, d), lambda i, h: (i, 0, 0)),
                  pl.BlockSpec((1, d, 4 * dim), lambda i, h: (h, 0, 0)),
                  pl.BlockSpec((1, 1, dim), lambda i, h: (h, 0, 0)),
                  pl.BlockSpec((1, dim), lambda i, h: (0, 0))],
        out_specs=pl.BlockSpec((1, s, dim), lambda i, h: (i, 0, h)),
        out_shape=jax.ShapeDtypeStruct((b, s, d), BF16),
        scratch_shapes=[pltpu.VMEM((chunk, 4 * dim), F32), pltpu.VMEM((chunk, 4 * dim), F32),
                        pltpu.VMEM((chunk // block, block, dim), F32),
                        pltpu.VMEM((chunk // block, block, dim), F32)],
        compiler_params=_params("parallel", "arbitrary"),
        name="hgrn_mixer",
    )(hn, w_heads, lb.reshape(HGRN_HEADS, 1, dim), out_gain.reshape(1, dim).astype(F32))


def _qk_kernel(a_ref, w_ref, g_ref, cos_ref, sin_ref, o_ref, *, tn):
    hd = DIFF_HEAD_DIM
    a = a_ref[...]
    cosf = cos_ref[...]
    sins = sin_ref[...]
    starts = list(range(0, tn, 2 * hd))
    nxt = jnp.dot(a, w_ref[:, 0:2 * hd], preferred_element_type=F32)
    for n_chunk, c0 in enumerate(starts):
        acc = nxt
        if n_chunk + 1 < len(starts):
            c1 = starts[n_chunk + 1]
            nxt = jnp.dot(a, w_ref[:, c1:c1 + 2 * hd], preferred_element_type=F32)
        for hh in range(2):
            x = acc[:, hh * hd:(hh + 1) * hd]
            sl = slice(c0 + hh * hd, c0 + (hh + 1) * hd)
            ms = jnp.mean(x * x, axis=-1, keepdims=True)
            y = x * lax.rsqrt(ms + NORM_EPS) * g_ref[:, sl]
            rot = pltpu.roll(y, hd // 2, 1)
            o_ref[:, sl] = (y * cosf + rot * sins).astype(o_ref.dtype)


def _qk_proj(hn, w, gains, cosf, sins, tm=256, tn=1024):
    n, d = hn.shape
    nout = w.shape[1]
    tm = min(tm, n)
    return pl.pallas_call(
        functools.partial(_qk_kernel, tn=tn),
        grid=(nout // tn, n // tm),
        in_specs=[pl.BlockSpec((tm, d), lambda j, i: (i, 0)),
                  pl.BlockSpec((d, tn), lambda j, i: (0, j)),
                  pl.BlockSpec((1, tn), lambda j, i: (0, j)),
                  pl.BlockSpec((tm, DIFF_HEAD_DIM), lambda j, i: (i, 0)),
                  pl.BlockSpec((tm, DIFF_HEAD_DIM), lambda j, i: (i, 0))],
        out_specs=pl.BlockSpec((tm, tn), lambda j, i: (i, j)),
        out_shape=jax.ShapeDtypeStruct((n, nout), BF16),
        compiler_params=_params("parallel", "arbitrary"),
        name="qk_proj",
    )(hn, w, gains, cosf, sins)


def _mm_bf16_kernel(a_ref, w_ref, o_ref):
    o_ref[...] = jnp.dot(a_ref[...], w_ref[...], preferred_element_type=F32).astype(o_ref.dtype)


def _matmul_bf16(a, w, tm=1024, tn=1024):
    n, k = a.shape
    nout = w.shape[1]
    tm, tn = min(tm, n), min(tn, nout)
    return pl.pallas_call(
        _mm_bf16_kernel,
        grid=(n // tm, nout // tn),
        in_specs=[pl.BlockSpec((tm, k), lambda i, j: (i, 0)), pl.BlockSpec((k, tn), lambda i, j: (0, j))],
        out_specs=pl.BlockSpec((tm, tn), lambda i, j: (i, j)),
        out_shape=jax.ShapeDtypeStruct((n, nout), BF16),
        compiler_params=_params("parallel", "parallel"),
        name="matmul_bf16",
    )(a, w)


def _attn_kernel(q_ref, k_ref, v_ref, lam_ref, sg_ref, o_ref, s_ref, m_ref, l_ref, acc_ref, *, seq, tile, lam_init):
    hd = DIFF_HEAD_DIM
    lf = lam_ref[...]
    lam = (jnp.exp(jnp.sum(lf[0:1] * lf[1:2], axis=-1, keepdims=True))
           - jnp.exp(jnp.sum(lf[2:3] * lf[3:4], axis=-1, keepdims=True)) + lam_init)
    row = lax.broadcasted_iota(jnp.int32, (tile, tile), 0)
    col = lax.broadcasted_iota(jnp.int32, (tile, tile), 1)
    contract_last = (((1,), (1,)), ((), ()))

    def lane_fold(x, op):
        out = x[:, :LANES]
        for t in range(1, tile // LANES):
            out = op(out, x[:, t * LANES:(t + 1) * LANES])
        return out

    def qblock(i, carry):
        q0 = pl.multiple_of(i * tile, tile)

        def score_tile(jb, c2):
            k0 = pl.multiple_of(jb * tile, tile)
            for c in range(2):
                q = q_ref[0, pl.ds(q0, tile), c * hd:(c + 1) * hd]
                kj = k_ref[0, pl.ds(k0, tile), c * hd:(c + 1) * hd]
                s = lax.dot_general(q, kj, contract_last, preferred_element_type=F32)
                s_ref[c, :, pl.ds(k0, tile)] = s
                m_ref[c] = jnp.maximum(m_ref[c], lane_fold(s, jnp.maximum))
            return c2

        for c in range(2):
            m_ref[c] = jnp.full((tile, LANES), -jnp.inf, F32)
        lax.fori_loop(0, i, score_tile, 0)
        for c in range(2):
            q = q_ref[0, pl.ds(q0, tile), c * hd:(c + 1) * hd]
            kd = k_ref[0, pl.ds(q0, tile), c * hd:(c + 1) * hd]
            s = lax.dot_general(q, kd, contract_last, preferred_element_type=F32)
            s = jnp.where(col <= row, s, -jnp.inf)
            s_ref[c, :, pl.ds(q0, tile)] = s
            m_part = jnp.maximum(m_ref[c], lane_fold(s, jnp.maximum))
            m_ref[c] = jnp.broadcast_to(jnp.max(m_part, axis=-1, keepdims=True), (tile, LANES))
            l_ref[c] = jnp.zeros((tile, LANES), F32)
            acc_ref[c] = jnp.zeros((tile, 2 * hd), F32)

        def value_tile(jb, c2):
            k0 = pl.multiple_of(jb * tile, tile)
            vj = v_ref[0, pl.ds(k0, tile), :]
            for c in range(2):
                m = m_ref[c]
                e = jnp.exp(s_ref[c, :, pl.ds(k0, tile)] - jnp.concatenate([m] * (tile // LANES), axis=1))
                l_ref[c] = l_ref[c] + lane_fold(e, jnp.add)
                acc_ref[c] = acc_ref[c] + jnp.dot(e.astype(BF16), vj, preferred_element_type=F32)
            return c2

        lax.fori_loop(0, i + 1, value_tile, 0)
        inv = [1.0 / jnp.sum(l_ref[c], axis=-1, keepdims=True) for c in range(2)]
        o = acc_ref[0] * inv[0] - lam * (acc_ref[1] * inv[1])
        ms = jnp.mean(o * o, axis=-1, keepdims=True)
        y = o * lax.rsqrt(ms + NORM_EPS) * sg_ref[...]
        o_ref[0, pl.ds(q0, tile), :] = (y * (1.0 - lam_init)).astype(o_ref.dtype)
        return carry

    lax.fori_loop(0, seq // tile, qblock, 0)


def _diff_attention(qk, v, lam, sub_gain, lam_init):
    b, s, _ = qk.shape
    hd = DIFF_HEAD_DIM
    tile = min(ATTN_TILE, s)
    nh = DIFF_HEADS
    return pl.pallas_call(
        functools.partial(_attn_kernel, seq=s, tile=tile, lam_init=lam_init),
        grid=(b, nh),
        in_specs=[pl.BlockSpec((1, s, 2 * hd), lambda i, h: (i, 0, h)),
                  pl.BlockSpec((1, s, 2 * hd), lambda i, h: (i, 0, nh + h)),
                  pl.BlockSpec((1, s, 2 * hd), lambda i, h: (i, 0, h)),
                  pl.BlockSpec((4, hd), lambda i, h: (0, 0)),
                  pl.BlockSpec((1, 2 * hd), lambda i, h: (0, 0))],
        out_specs=pl.BlockSpec((1, s, 2 * hd), lambda i, h: (i, 0, h)),
        out_shape=jax.ShapeDtypeStruct((b, s, nh * 2 * hd), BF16),
        scratch_shapes=[pltpu.VMEM((2, tile, s), F32), pltpu.VMEM((2, tile, LANES), F32),
                        pltpu.VMEM((2, tile, LANES), F32), pltpu.VMEM((2, tile, 2 * hd), F32)],
        compiler_params=_params("parallel", "parallel"),
        name="diff_attention",
    )(qk, qk, v, lam.astype(F32), sub_gain.reshape(1, 2 * hd).astype(F32))


def _router_kernel(h_ref, g_ref, w_ref, xn_ref, meta_ref, cnt_ref):
    x = h_ref[...]
    ms = jnp.mean(x * x, axis=-1, keepdims=True)
    xn = x * lax.rsqrt(ms + NORM_EPS) * g_ref[...]
    xn_ref[...] = xn
    x_hi = xn.astype(BF16)
    x_lo = (xn - x_hi.astype(F32)).astype(BF16)
    both = jnp.dot(x_hi, w_ref[...], preferred_element_type=F32)
    lg = both[:, :LANES] + both[:, LANES:] + jnp.dot(x_lo, w_ref[:, :LANES], preferred_element_type=F32)
    lane = lax.broadcasted_iota(jnp.int32, lg.shape, 1)
    is_group = lane < N_GROUPS
    gmax = jnp.max(jnp.where(is_group, lg, -jnp.inf), axis=-1, keepdims=True)
    ge = jnp.where(is_group, jnp.exp(lg - gmax), 0.0)
    gp = ge / jnp.sum(ge, axis=-1, keepdims=True)
    g_top_p = jnp.max(jnp.where(is_group, gp, -1.0), axis=-1, keepdims=True)
    g_top = jnp.min(jnp.where(is_group & (gp == g_top_p), lane, LANES), axis=-1, keepdims=True)
    first = N_GROUPS + g_top * EXPERTS_PER_GROUP
    in_group = (lane >= first) & (lane < first + EXPERTS_PER_GROUP)
    emax = jnp.max(jnp.where(in_group, lg, -jnp.inf), axis=-1, keepdims=True)
    ee = jnp.where(in_group, jnp.exp(lg - emax), 0.0)
    ep = ee / jnp.sum(ee, axis=-1, keepdims=True)
    p1 = jnp.max(jnp.where(in_group, ep, -1.0), axis=-1, keepdims=True)
    i1 = jnp.min(jnp.where(in_group & (ep == p1), lane, LANES), axis=-1, keepdims=True)
    rest = in_group & (lane != i1)
    p2 = jnp.max(jnp.where(rest, ep, -1.0), axis=-1, keepdims=True)
    i2 = jnp.min(jnp.where(rest & (ep == p2), lane, LANES), axis=-1, keepdims=True)
    denom = p1 + p2
    gate1 = g_top_p * p1 / denom
    gate2 = g_top_p * p2 / denom
    e1 = (i1 - N_GROUPS).astype(F32)
    e2 = (i2 - N_GROUPS).astype(F32)

    @pl.when(pl.program_id(0) == 0)
    def _():
        cnt_ref[...] = jnp.zeros_like(cnt_ref)

    chosen = jnp.where((lane == i1) | (lane == i2), 1.0, 0.0)
    cnt_ref[...] = cnt_ref[...] + jnp.sum(chosen, axis=0, keepdims=True)
    meta_ref[...] = jnp.where(lane == 0, e1, jnp.where(lane == 1, e2, jnp.where(lane == 2, gate1,
                              jnp.where(lane == 3, gate2, 0.0))))


def _router(h, gain, w_split, tm=256):
    n, d = h.shape
    tm = min(tm, n)
    return pl.pallas_call(
        _router_kernel,
        grid=(n // tm,),
        in_specs=[pl.BlockSpec((tm, d), lambda i: (i, 0)),
                  pl.BlockSpec((1, d), lambda i: (0, 0)),
                  pl.BlockSpec((d, 2 * LANES), lambda i: (0, 0))],
        out_specs=[pl.BlockSpec((tm, d), lambda i: (i, 0)), pl.BlockSpec((tm, LANES), lambda i: (i, 0)),
                   pl.BlockSpec((1, LANES), lambda i: (0, 0))],
        out_shape=[jax.ShapeDtypeStruct((n, d), F32), jax.ShapeDtypeStruct((n, LANES), F32),
                   jax.ShapeDtypeStruct((1, LANES), F32)],
        compiler_params=_params("arbitrary"),
        name="moe_router",
    )(h, gain.reshape(1, d).astype(F32), w_split)


def _row_copy(src_ref, dst_ref, src_row, dst_row, sem):
    return pltpu.make_async_copy(src_ref.at[pl.ds(src_row, 1)], dst_ref.at[pl.ds(dst_row, 1)], sem)


def _expert_kernel(be_ref, nb_ref, tok_ref, dst_ref, par_ref, nxt_ref,
                   xn_hbm, wg_hbm, wu_hbm, wd_hbm, y_hbm,
                   xbuf, ybuf, xsem, ysem, wg32, wu32, wd32, wsem, wg16, wu16, wd16, *, rows):
    i = pl.program_id(0)
    n_used = nb_ref[0]
    slot = lax.rem(i, RESULT_SLOTS)
    prev = lax.rem(i + RESULT_SLOTS - 1, RESULT_SLOTS)
    xslot = lax.rem(i, GATHER_SLOTS)
    stages = ((wg_hbm, wg32), (wu_hbm, wu32), (wd_hbm, wd32))

    def weights(expert, s):
        return [pltpu.make_async_copy(hbm.at[expert], stage.at[s], wsem.at[s]) for hbm, stage in stages]

    def gather(block, into):
        for r in range(rows):
            _row_copy(xn_hbm, xbuf.at[into], tok_ref[block * rows + r], r, xsem.at[into]).start(ROW_DMA_PRIORITY)

    def scatter(block, frm):
        for r in range(rows):
            _row_copy(ybuf.at[frm], y_hbm, r, dst_ref[block * rows + r], ysem.at[frm]).start(ROW_DMA_PRIORITY)

    def wait_gather(s):
        pltpu.make_async_copy(xn_hbm.at[pl.ds(0, rows)], xbuf.at[s], xsem.at[s]).wait()

    def wait_scatter(s):
        pltpu.make_async_copy(ybuf.at[s], y_hbm.at[pl.ds(0, rows)], ysem.at[s]).wait()

    def ahead(k):
        return jnp.minimum(i + k, n_used - 1)

    def mlp():
        x = xbuf[xslot].astype(BF16)
        g = jnp.dot(x, wg16[...], preferred_element_type=F32)
        u = jnp.dot(x, wu16[...], preferred_element_type=F32)
        hmid = (g * _sigmoid(g)) * u
        ybuf[slot] = jnp.dot(hmid.astype(BF16), wd16[...], preferred_element_type=F32)

    @pl.when(i == 0)
    def _():
        for c in weights(be_ref[0], 0):
            c.start(WEIGHT_DMA_PRIORITY)
        for k in range(GATHER_SLOTS - 1):
            gather(ahead(k), k)
        ybuf[1] = jnp.zeros((rows, ybuf.shape[2]), F32)
        n_real = y_hbm.shape[0] - RESULT_SLOTS * rows
        clears = [pltpu.make_async_copy(ybuf.at[1], y_hbm.at[pl.ds(n_real + p * rows, rows)], ysem.at[1])
                  for p in range(RESULT_SLOTS)]
        for c in clears:
            c.start()
        for c in clears:
            c.wait()

    new_expert = jnp.logical_or(i == 0, be_ref[i] != be_ref[jnp.maximum(i - 1, 0)])

    @pl.when(jnp.logical_and(i < n_used, new_expert))
    def _():
        s = par_ref[i]
        for c in weights(0, s):
            c.wait()

        @pl.when(nxt_ref[i] >= 0)
        def _():
            for c in weights(nxt_ref[i], 1 - s):
                c.start(WEIGHT_DMA_PRIORITY)

        wg16[...] = wg32[s].astype(BF16)
        wu16[...] = wu32[s].astype(BF16)
        wd16[...] = wd32[s].astype(BF16)

    @pl.when(i == 0)
    def _():
        wait_gather(xslot)
        gather(ahead(GATHER_SLOTS - 1), lax.rem(i + GATHER_SLOTS - 1, GATHER_SLOTS))
        mlp()

    @pl.when(jnp.logical_and(i >= 1, i < n_used))
    def _():
        @pl.when(i >= RESULT_SLOTS)
        def _():
            wait_scatter(slot)

        wait_gather(xslot)
        gather(ahead(GATHER_SLOTS - 1), lax.rem(i + GATHER_SLOTS - 1, GATHER_SLOTS))
        scatter(i - 1, prev)
        mlp()

    @pl.when(i == n_used)
    def _():
        scatter(i - 1, prev)
        for k in range(1, RESULT_SLOTS + 1):
            @pl.when(i >= k)
            def _():
                wait_scatter(lax.rem(i + RESULT_SLOTS - k, RESULT_SLOTS))

        for k in range(GATHER_SLOTS - 1):
            wait_gather(lax.rem(i + k, GATHER_SLOTS))


def _expert_mlp(xn, row_tok, row_dst, block_expert, n_used, parity, next_expert, w_gate, w_up, w_down, n_out,
                rows=EXPERT_ROWS):
    d = xn.shape[1]
    n_rows = row_tok.shape[0]
    ff = w_gate.shape[-1]
    any_spec = pl.BlockSpec(memory_space=pl.ANY)
    return pl.pallas_call(
        functools.partial(_expert_kernel, rows=rows),
        grid_spec=pltpu.PrefetchScalarGridSpec(
            num_scalar_prefetch=6,
            grid=(n_rows // rows,),
            in_specs=[any_spec, any_spec, any_spec, any_spec],
            out_specs=any_spec,
            scratch_shapes=[pltpu.VMEM((GATHER_SLOTS, rows, d), F32), pltpu.VMEM((RESULT_SLOTS, rows, d), F32),
                            pltpu.SemaphoreType.DMA((GATHER_SLOTS,)), pltpu.SemaphoreType.DMA((RESULT_SLOTS,)),
                            pltpu.VMEM((2, d, ff), F32), pltpu.VMEM((2, d, ff), F32), pltpu.VMEM((2, ff, d), F32),
                            pltpu.SemaphoreType.DMA((2,)),
                            pltpu.VMEM((d, ff), BF16), pltpu.VMEM((d, ff), BF16), pltpu.VMEM((ff, d), BF16)]),
        out_shape=jax.ShapeDtypeStruct((n_out, d), F32),
        compiler_params=_params("arbitrary"),
        name="moe_experts",
    )(block_expert, n_used, row_tok, row_dst, parity, next_expert, xn, w_gate, w_up, w_down)


def _combine_kernel(*refs, with_norm):
    if with_norm:
        h_ref, meta_ref, y0_ref, y1_ref, g_ref, o_ref, hn_ref = refs
    else:
        h_ref, meta_ref, y0_ref, y1_ref, o_ref = refs
    meta = meta_ref[...]
    out = h_ref[...] + (y0_ref[...] * meta[:, 2:3] + y1_ref[...] * meta[:, 3:4])
    o_ref[...] = out
    if with_norm:
        ms = jnp.mean(out * out, axis=-1, keepdims=True)
        hn_ref[...] = (out * lax.rsqrt(ms + NORM_EPS) * g_ref[...]).astype(hn_ref.dtype)


def _combine(h, meta, y2, next_gain=None, tm=256):
    n, d = h.shape
    tm = min(tm, n)
    with_norm = next_gain is not None
    row_spec = pl.BlockSpec((tm, d), lambda i: (i, 0))
    in_specs = [row_spec, pl.BlockSpec((tm, LANES), lambda i: (i, 0)), row_spec,
                pl.BlockSpec((tm, d), lambda i: (n // tm + i, 0))]
    args = [h, meta, y2, y2]
    out_specs, out_shape = [row_spec], [jax.ShapeDtypeStruct((n, d), F32)]
    if with_norm:
        in_specs.append(pl.BlockSpec((1, d), lambda i: (0, 0)))
        args.append(next_gain.reshape(1, d).astype(F32))
        out_specs.append(row_spec)
        out_shape.append(jax.ShapeDtypeStruct((n, d), BF16))
    outs = pl.pallas_call(
        functools.partial(_combine_kernel, with_norm=with_norm),
        grid=(n // tm,),
        in_specs=in_specs,
        out_specs=out_specs,
        out_shape=out_shape,
        compiler_params=_params("parallel"),
        name="moe_combine",
    )(*args)
    return (outs[0], outs[1]) if with_norm else (outs[0], None)


def _split_router_weights(w_group, w_expert):
    d = w_group.shape[0]
    w = jnp.concatenate([w_group.astype(F32), w_expert.astype(F32),
                         jnp.zeros((d, LANES - N_GROUPS - N_EXPERTS), F32)], axis=1)
    hi = w.astype(BF16)
    lo = (w - hi.astype(F32)).astype(BF16)
    return jnp.concatenate([hi, lo], axis=1)


def _hier_moe(h, ffn_gain, w_group, w_expert, w_gate, w_up, w_down, layer, next_gain=None):
    n, d = h.shape
    rows = EXPERT_ROWS
    xn, meta, cnt = _router(h, ffn_gain, _split_router_weights(w_group, w_expert))
    ids = jnp.arange(N_EXPERTS, dtype=jnp.int32)
    expert = meta[:, :TOP_K].astype(jnp.int32).reshape(-1)
    counts = cnt[0, N_GROUPS:N_GROUPS + N_EXPERTS].astype(jnp.int32)
    n_assign = n * TOP_K
    n_blocks = n_assign // rows + N_EXPERTS
    n_rows = n_blocks * rows
    order = jnp.argsort(expert, stable=True).astype(jnp.int32)
    start = jnp.cumsum(counts) - counts
    pad_counts = (counts + rows - 1) // rows * rows
    pad_end = jnp.cumsum(pad_counts)
    pad_start = pad_end - pad_counts
    n_used = (pad_end[-1] // rows).astype(jnp.int32).reshape(1)
    blk = jnp.minimum(jnp.arange(n_blocks, dtype=jnp.int32), n_used[0] - 1)
    block_expert = jnp.sum((pad_end[None, :] <= (blk * rows)[:, None]).astype(jnp.int32), axis=1)
    block_expert = jnp.minimum(block_expert, N_EXPERTS - 1)
    b_idx = jnp.arange(n_blocks, dtype=jnp.int32)
    first = b_idx * rows - pad_start[block_expert]
    src = jnp.clip(start[block_expert] + first, 0, n_assign)
    order_padded = jnp.concatenate([order, jnp.zeros((rows,), jnp.int32)])
    choice = jax.vmap(lambda s0: lax.dynamic_slice(order_padded, (s0,), (rows,)))(src)
    j = jnp.arange(rows, dtype=jnp.int32)[None, :]
    real = (b_idx < n_used[0])[:, None] & (first[:, None] + j < counts[block_expert][:, None])
    spare = TOP_K * n + (b_idx % RESULT_SLOTS)[:, None] * rows + j
    row_tok = jnp.where(real, choice // TOP_K, 0).reshape(-1)
    row_dst = jnp.where(real, (choice % TOP_K) * n + choice // TOP_K, spare).reshape(-1)
    nonempty = counts > 0
    position = jnp.cumsum(nonempty.astype(jnp.int32)) - 1
    later = jnp.where(nonempty[None, :] & (ids[None, :] > ids[:, None]), ids[None, :], N_EXPERTS)
    following = jnp.min(later, axis=1)
    following = jnp.where(following < N_EXPERTS, following + layer * N_EXPERTS, -1)
    y2 = _expert_mlp(xn, row_tok, row_dst, block_expert + layer * N_EXPERTS, n_used,
                     position[block_expert] % 2, following[block_expert], w_gate, w_up, w_down,
                     TOP_K * n + RESULT_SLOTS * rows, rows)
    return _combine(h, meta, y2, next_gain)


def kernel(x, positions, mix_norm, ffn_norm, hgrn_w_in, hgrn_out_gain, hgrn_w_out, hgrn_lb_logits, diff_w_in, diff_q_gain, diff_k_gain, diff_lambda, diff_sub_gain, diff_w_out, moe_w_group, moe_w_expert, moe_w_gate, moe_w_up, moe_w_down):
    b, s, d = x.shape
    n = b * s
    depth = mix_norm.shape[0]
    lower_bounds = jnp.cumsum(jax.nn.softmax(hgrn_lb_logits.astype(F32), axis=0), axis=0)
    h = x.reshape(n, d)
    w_gate = moe_w_gate.reshape((-1,) + moe_w_gate.shape[2:])
    w_up = moe_w_up.reshape((-1,) + moe_w_up.shape[2:])
    w_down = moe_w_down.reshape((-1,) + moe_w_down.shape[2:])
    hn = _rmsnorm_bf16(h, mix_norm[0])
    for layer in range(depth):
        j = layer // N_MIXERS
        if layer % N_MIXERS == 0:
            w_heads = (hgrn_w_in[j].reshape(d, 4, HGRN_HEADS, HGRN_DIM).transpose(2, 0, 1, 3)
                       .reshape(HGRN_HEADS, d, 4 * HGRN_DIM).astype(BF16))
            o = _hgrn_mixer(hn.reshape(b, s, d), w_heads, lower_bounds[layer], hgrn_out_gain[j])
            h = _matmul_residual(o.reshape(n, d), hgrn_w_out[j].astype(BF16), h)
        else:
            lam_init = 0.8 - 0.6 * math.exp(-0.3 * layer)
            hd = DIFF_HEAD_DIM
            half = hd // 2
            inv_freq = ROPE_THETA ** (-jnp.arange(half, dtype=F32) / half)
            ang = positions.astype(F32).reshape(n, 1) * inv_freq[None, :]
            cosf = jnp.concatenate([jnp.cos(ang), jnp.cos(ang)], axis=1)
            sins = jnp.concatenate([-jnp.sin(ang), jnp.sin(ang)], axis=1)
            n_heads = 2 * DIFF_HEADS
            n_qk = 2 * n_heads * hd
            gains = jnp.concatenate([jnp.tile(diff_q_gain[j].astype(F32), n_heads) * (hd ** -0.5),
                                     jnp.tile(diff_k_gain[j].astype(F32), n_heads)]).reshape(1, -1)
            w_in = diff_w_in[j].astype(BF16)
            qk = _qk_proj(hn, w_in[:, :n_qk], gains, cosf, sins)
            v = _matmul_bf16(hn, w_in[:, n_qk:])
            o = _diff_attention(qk.reshape(b, s, -1), v.reshape(b, s, -1), diff_lambda[j], diff_sub_gain[j],
                                lam_init)
            h = _matmul_residual(o.reshape(n, d), diff_w_out[j].astype(BF16), h)
        next_gain = mix_norm[layer + 1] if layer + 1 < depth else None
        h, hn = _hier_moe(h, ffn_norm[layer], moe_w_group[layer], moe_w_expert[layer], w_gate, w_up, w_down, layer,
                          next_gain)
    return h.reshape(b, s, d)
```

```python
import functools
import math

import jax
import jax.numpy as jnp
import numpy as np
from jax import lax
from jax.experimental import pallas as pl
from jax.experimental.pallas import tpu as pltpu

F32 = jnp.float32
BF16 = jnp.bfloat16

NORM_EPS = 1e-6
N_MIXERS = 2
HGRN_HEADS = 16
HGRN_DIM = 128
HGRN_BLOCK = 128
HGRN_CHUNK = 512
HGRN_SUB = 16
DIFF_HEADS = 8
DIFF_HEAD_DIM = 128
ROPE_THETA = 10000.0
ATTN_TILE = 256
N_GROUPS = 8
EXPERTS_PER_GROUP = 8
N_EXPERTS = N_GROUPS * EXPERTS_PER_GROUP
TOP_K = 2
EXPERT_ROWS = 256
GATHER_SLOTS = 3
RESULT_SLOTS = 3
ROW_DMA_PRIORITY = 0
WEIGHT_DMA_PRIORITY = 1
LANES = 128
VMEM_LIMIT_BYTES = 56 * 1024 * 1024


def _params(*semantics):
    return pltpu.CompilerParams(dimension_semantics=semantics, vmem_limit_bytes=VMEM_LIMIT_BYTES)


def _norm_kernel(h_ref, g_ref, o_ref):
    x = h_ref[...]
    ms = jnp.mean(x * x, axis=-1, keepdims=True)
    o_ref[...] = (x * lax.rsqrt(ms + NORM_EPS) * g_ref[...]).astype(o_ref.dtype)


def _rmsnorm_bf16(h, gain, tm=512):
    n, d = h.shape
    return pl.pallas_call(
        _norm_kernel,
        grid=(n // tm,),
        in_specs=[pl.BlockSpec((tm, d), lambda i: (i, 0)), pl.BlockSpec((1, d), lambda i: (0, 0))],
        out_specs=pl.BlockSpec((tm, d), lambda i: (i, 0)),
        out_shape=jax.ShapeDtypeStruct((n, d), BF16),
        compiler_params=_params("parallel"),
        name="rmsnorm_bf16",
    )(h, gain.reshape(1, d).astype(F32))


def _mm_res_kernel(a_ref, w_ref, r_ref, o_ref):
    o_ref[...] = r_ref[...] + jnp.dot(a_ref[...], w_ref[...], preferred_element_type=F32)


def _matmul_residual(a, w, res, tm=1024, tn=1024):
    n, k = a.shape
    nout = w.shape[1]
    tm, tn = min(tm, n), min(tn, nout)
    return pl.pallas_call(
        _mm_res_kernel,
        grid=(n // tm, nout // tn),
        in_specs=[pl.BlockSpec((tm, k), lambda i, j: (i, 0)),
                  pl.BlockSpec((k, tn), lambda i, j: (0, j)),
                  pl.BlockSpec((tm, tn), lambda i, j: (i, j))],
        out_specs=pl.BlockSpec((tm, tn), lambda i, j: (i, j)),
        out_shape=jax.ShapeDtypeStruct((n, nout), F32),
        compiler_params=_params("parallel", "parallel"),
        name="matmul_residual",
    )(a, w, res)


def _sigmoid(x):
    return 1.0 / (1.0 + jnp.exp(-x))


def _ref_rows(a_ref, half, block):
    span = 2 * half
    parts = [jnp.broadcast_to(a_ref[g * span + half - 1:g * span + half, :], (span, a_ref.shape[1]))
             for g in range(block // span)]
    return parts[0] if len(parts) == 1 else jnp.concatenate(parts, axis=0)


def _cumsum_rows(x, row):
    shift = 1
    while shift < x.shape[0]:
        x = x + jnp.where(row >= shift, pltpu.roll(x, shift, 0), 0.0)
        shift *= 2
    return x


def _hgrn_kernel(hn_ref, w_ref, lb_ref, og_ref, o_ref, proj_a, proj_b, cum_a, cum_b, *, seq, block, chunk):
    dim = HGRN_DIM
    blocks_per_chunk = chunk // block
    n_chunks = seq // chunk
    lb = lb_ref[0]
    og = og_ref[...]
    t_idx = lax.broadcasted_iota(jnp.int32, (block, block), 0)
    s_idx = lax.broadcasted_iota(jnp.int32, (block, block), 1)
    causal = s_idx <= t_idx
    split = t_idx ^ s_idx
    row = lax.broadcasted_iota(jnp.int32, (block, dim), 0)
    halves = []
    half = block // 2
    while half >= HGRN_SUB:
        halves.append(half)
        half //= 2
    contract_last = (((1,), (1,)), ((), ()))
    contract_first = (((0,), (0,)), ((), ()))

    def project(c, buf):
        r0 = pl.multiple_of(c * chunk, chunk)
        buf[...] = jnp.dot(hn_ref[0, pl.ds(r0, chunk), :], w_ref[0], preferred_element_type=F32)

    def one_block(buf, cum_ref, j, r0, state_t):
        rows = slice(j * block, (j + 1) * block)
        pq = buf[rows, 0 * dim:1 * dim]
        pf = buf[rows, 1 * dim:2 * dim]
        v16 = buf[rows, 2 * dim:3 * dim].astype(BF16)
        pg = buf[rows, 3 * dim:4 * dim]
        q = pq * _sigmoid(pq)
        forget = lb + (1.0 - lb) * _sigmoid(pf)
        k = 1.0 - forget
        cum = _cumsum_rows(jnp.log(forget), row)
        cum_ref[j] = cum
        blk = cum_ref.at[j]
        total = blk[block - 1:block, :]
        scores = None
        for half in halves:
            e = jnp.exp(-jnp.abs(cum - _ref_rows(blk, half, block)))
            p = lax.dot_general((q * e).astype(BF16), (k * e).astype(BF16), contract_last,
                                preferred_element_type=F32)
            scores = p if scores is None else jnp.where(split >= 2 * half, scores, p)
        sub_ref = jnp.concatenate(
            [jnp.zeros((HGRN_SUB, dim), F32)]
            + [jnp.broadcast_to(blk[s * HGRN_SUB - 1:s * HGRN_SUB, :], (HGRN_SUB, dim))
               for s in range(1, block // HGRN_SUB)], axis=0)
        loc = cum - sub_ref
        p = lax.dot_general((q * jnp.exp(loc)).astype(BF16), (k * jnp.exp(-loc)).astype(BF16),
                            contract_last, preferred_element_type=F32)
        scores = p if scores is None else jnp.where(split >= HGRN_SUB, scores, p)
        scores = jnp.where(causal, scores, 0.0)
        o = jnp.dot(scores.astype(BF16), v16, preferred_element_type=F32)
        o = o + lax.dot_general((q * jnp.exp(cum)).astype(BF16), state_t.astype(BF16), contract_last,
                                preferred_element_type=F32)
        k_end = (k * jnp.exp(total - cum)).astype(BF16)
        state_t = state_t * jnp.exp(total) + lax.dot_general(v16, k_end, contract_first,
                                                              preferred_element_type=F32)
        ms = jnp.mean(o * o, axis=-1, keepdims=True)
        y = o * lax.rsqrt(ms + NORM_EPS) * og
        o_ref[0, pl.ds(r0 + j * block, block), :] = (y * (pg * _sigmoid(pg))).astype(o_ref.dtype)
        return state_t

    bufs = (proj_a, proj_b)
    cum_refs = (cum_a, cum_b)
    project(0, bufs[0])

    def chunk_pair(cp, state_t):
        for parity in range(2):
            c = 2 * cp + parity
            project(jnp.minimum(c + 1, n_chunks - 1), bufs[1 - parity])
            r0 = pl.multiple_of(c * chunk, chunk)
            for j in range(blocks_per_chunk):
                state_t = one_block(bufs[parity], cum_refs[parity], j, r0, state_t)
        return state_t

    state = lax.fori_loop(0, n_chunks // 2, chunk_pair, jnp.zeros((dim, dim), F32))
    if n_chunks % 2:
        r0 = (n_chunks - 1) * chunk
        for j in range(blocks_per_chunk):
            state = one_block(bufs[0], cum_refs[0], j, r0, state)


def _hgrn_mixer(hn, w_heads, lb, out_gain):
    b, s, d = hn.shape
    dim = HGRN_DIM
    block = min(HGRN_BLOCK, s)
    chunk = min(HGRN_CHUNK, s)
    return pl.pallas_call(
        functools.partial(_hgrn_kernel, seq=s, block=block, chunk=chunk),
        grid=(b, HGRN_HEADS),
        in_specs=[pl.BlockSpec((1, s, d), lambda i, h: (i, 0, 0)),
                  pl.BlockSpec((1, d, 4 * dim), lambda i, h: (h, 0, 0)),
                  pl.BlockSpec((1, 1, dim), lambda i, h: (h, 0, 0)),
                  pl.BlockSpec((1, dim), lambda i, h: (0, 0))],
        out_specs=pl.BlockSpec((1, s, dim), lambda i, h: (i, 0, h)),
        out_shape=jax.ShapeDtypeStruct((b, s, d), BF16),
        scratch_shapes=[pltpu.VMEM((chunk, 4 * dim), F32), pltpu.VMEM((chunk, 4 * dim), F32),
                        pltpu.VMEM((chunk // block, block, dim), F32),
                        pltpu.VMEM((chunk // block, block, dim), F32)],
        compiler_params=_params("parallel", "arbitrary"),
        name="hgrn_mixer",
    )(hn, w_heads, lb.reshape(HGRN_HEADS, 1, dim), out_gain.reshape(1, dim).astype(F32))


def _qk_kernel(a_ref, w_ref, g_ref, cos_ref, sin_ref, o_ref, *, tn):
    hd = DIFF_HEAD_DIM
    a = a_ref[...]
    cosf = cos_ref[...]
    sins = sin_ref[...]
    starts = list(range(0, tn, 2 * hd))
    nxt = jnp.dot(a, w_ref[:, 0:2 * hd], preferred_element_type=F32)
    for n_chunk, c0 in enumerate(starts):
        acc = nxt
        if n_chunk + 1 < len(starts):
            c1 = starts[n_chunk + 1]
            nxt = jnp.dot(a, w_ref[:, c1:c1 + 2 * hd], preferred_element_type=F32)
        for hh in range(2):
            x = acc[:, hh * hd:(hh + 1) * hd]
            sl = slice(c0 + hh * hd, c0 + (hh + 1) * hd)
            ms = jnp.mean(x * x, axis=-1, keepdims=True)
            y = x * lax.rsqrt(ms + NORM_EPS) * g_ref[:, sl]
            rot = pltpu.roll(y, hd // 2, 1)
            o_ref[:, sl] = (y * cosf + rot * sins).astype(o_ref.dtype)


def _qk_proj(hn, w, gains, cosf, sins, tm=256, tn=1024):
    n, d = hn.shape
    nout = w.shape[1]
    tm = min(tm, n)
    return pl.pallas_call(
        functools.partial(_qk_kernel, tn=tn),
        grid=(nout // tn, n // tm),
        in_specs=[pl.BlockSpec((tm, d), lambda j, i: (i, 0)),
                  pl.BlockSpec((d, tn), lambda j, i: (0, j)),
                  pl.BlockSpec((1, tn), lambda j, i: (0, j)),
                  pl.BlockSpec((tm, DIFF_HEAD_DIM), lambda j, i: (i, 0)),
                  pl.BlockSpec((tm, DIFF_HEAD_DIM), lambda j, i: (i, 0))],
        out_specs=pl.BlockSpec((tm, tn), lambda j, i: (i, j)),
        out_shape=jax.ShapeDtypeStruct((n, nout), BF16),
        compiler_params=_params("parallel", "arbitrary"),
        name="qk_proj",
    )(hn, w, gains, cosf, sins)


def _mm_bf16_kernel(a_ref, w_ref, o_ref):
    o_ref[...] = jnp.dot(a_ref[...], w_ref[...], preferred_element_type=F32).astype(o_ref.dtype)


def _matmul_bf16(a, w, tm=1024, tn=1024):
    n, k = a.shape
    nout = w.shape[1]
    tm, tn = min(tm, n), min(tn, nout)
    return pl.pallas_call(
        _mm_bf16_kernel,
        grid=(n // tm, nout // tn),
        in_specs=[pl.BlockSpec((tm, k), lambda i, j: (i, 0)), pl.BlockSpec((k, tn), lambda i, j: (0, j))],
        out_specs=pl.BlockSpec((tm, tn), lambda i, j: (i, j)),
        out_shape=jax.ShapeDtypeStruct((n, nout), BF16),
        compiler_params=_params("parallel", "parallel"),
        name="matmul_bf16",
    )(a, w)


def _attn_kernel(q_ref, k_ref, v_ref, lam_ref, sg_ref, o_ref, kt_ref, s_ref, m_ref, l_ref, acc_ref, *, seq, tile,
                 lam_init):
    hd = DIFF_HEAD_DIM
    lf = lam_ref[...]
    lam = (jnp.exp(jnp.sum(lf[0:1] * lf[1:2], axis=-1, keepdims=True))
           - jnp.exp(jnp.sum(lf[2:3] * lf[3:4], axis=-1, keepdims=True)) + lam_init)
    row = lax.broadcasted_iota(jnp.int32, (tile, tile), 0)
    col = lax.broadcasted_iota(jnp.int32, (tile, tile), 1)

    for c in range(2):
        for t in range(seq // tile):
            kt = k_ref[0, t * tile:(t + 1) * tile, c * hd:(c + 1) * hd].astype(F32).T
            kt_ref[c, :, t * tile:(t + 1) * tile] = kt.astype(BF16)

    def lane_fold(x, op):
        out = x[:, :LANES]
        for t in range(1, x.shape[1] // LANES):
            out = op(out, x[:, t * LANES:(t + 1) * LANES])
        return out

    def qblock(i, carry):
        q0 = pl.multiple_of(i * tile, tile)
        odd = lax.rem(i, 2)

        def score(k0, width):
            for c in range(2):
                q = q_ref[0, pl.ds(q0, tile), c * hd:(c + 1) * hd]
                s = jnp.dot(q, kt_ref[c, :, pl.ds(k0, width)], preferred_element_type=F32)
                s_ref[c, :, pl.ds(k0, width)] = s
                m_ref[c] = jnp.maximum(m_ref[c], lane_fold(s, jnp.maximum))

        def value(k0, width):
            vj = v_ref[0, pl.ds(k0, width), :]
            for c in range(2):
                m = m_ref[c]
                e = jnp.exp(s_ref[c, :, pl.ds(k0, width)] - jnp.concatenate([m] * (width // LANES), axis=1))
                l_ref[c] = l_ref[c] + lane_fold(e, jnp.add)
                acc_ref[c] = acc_ref[c] + jnp.dot(e.astype(BF16), vj, preferred_element_type=F32)

        for c in range(2):
            m_ref[c] = jnp.full((tile, LANES), -jnp.inf, F32)

        @pl.when(odd == 1)
        def _():
            score(0, tile)

        def score_pair(p, c2):
            score(pl.multiple_of(odd * tile + p * 2 * tile, tile), 2 * tile)
            return c2

        lax.fori_loop(0, i // 2, score_pair, 0)
        for c in range(2):
            q = q_ref[0, pl.ds(q0, tile), c * hd:(c + 1) * hd]
            s = jnp.dot(q, kt_ref[c, :, pl.ds(q0, tile)], preferred_element_type=F32)
            s = jnp.where(col <= row, s, -jnp.inf)
            s_ref[c, :, pl.ds(q0, tile)] = s
            m_part = jnp.maximum(m_ref[c], lane_fold(s, jnp.maximum))
            m_ref[c] = jnp.broadcast_to(jnp.max(m_part, axis=-1, keepdims=True), (tile, LANES))
            l_ref[c] = jnp.zeros((tile, LANES), F32)
            acc_ref[c] = jnp.zeros((tile, 2 * hd), F32)

        @pl.when(odd == 0)
        def _():
            value(0, tile)

        def value_pair(p, c2):
            value(pl.multiple_of((1 - odd) * tile + p * 2 * tile, tile), 2 * tile)
            return c2

        lax.fori_loop(0, (i + 1) // 2, value_pair, 0)
        inv = [1.0 / jnp.sum(l_ref[c], axis=-1, keepdims=True) for c in range(2)]
        o = acc_ref[0] * inv[0] - lam * (acc_ref[1] * inv[1])
        ms = jnp.mean(o * o, axis=-1, keepdims=True)
        y = o * lax.rsqrt(ms + NORM_EPS) * sg_ref[...]
        o_ref[0, pl.ds(q0, tile), :] = (y * (1.0 - lam_init)).astype(o_ref.dtype)
        return carry

    lax.fori_loop(0, seq // tile, qblock, 0)


def _diff_attention(qk, v, lam, sub_gain, lam_init):
    b, s, _ = qk.shape
    hd = DIFF_HEAD_DIM
    tile = min(ATTN_TILE, s)
    nh = DIFF_HEADS
    return pl.pallas_call(
        functools.partial(_attn_kernel, seq=s, tile=tile, lam_init=lam_init),
        grid=(b, nh),
        in_specs=[pl.BlockSpec((1, s, 2 * hd), lambda i, h: (i, 0, h)),
                  pl.BlockSpec((1, s, 2 * hd), lambda i, h: (i, 0, nh + h)),
                  pl.BlockSpec((1, s, 2 * hd), lambda i, h: (i, 0, h)),
                  pl.BlockSpec((4, hd), lambda i, h: (0, 0)),
                  pl.BlockSpec((1, 2 * hd), lambda i, h: (0, 0))],
        out_specs=pl.BlockSpec((1, s, 2 * hd), lambda i, h: (i, 0, h)),
        out_shape=jax.ShapeDtypeStruct((b, s, nh * 2 * hd), BF16),
        scratch_shapes=[pltpu.VMEM((2, hd, s), BF16), pltpu.VMEM((2, tile, s), F32), pltpu.VMEM((2, tile, LANES), F32),
                        pltpu.VMEM((2, tile, LANES), F32), pltpu.VMEM((2, tile, 2 * hd), F32)],
        compiler_params=_params("parallel", "parallel"),
        name="diff_attention",
    )(qk, qk, v, lam.astype(F32), sub_gain.reshape(1, 2 * hd).astype(F32))


def _router_kernel(h_ref, g_ref, w_ref, xn_ref, meta_ref, cnt_ref):
    x = h_ref[...]
    ms = jnp.mean(x * x, axis=-1, keepdims=True)
    xn = x * lax.rsqrt(ms + NORM_EPS) * g_ref[...]
    xn_ref[...] = xn
    x_hi = xn.astype(BF16)
    x_lo = (xn - x_hi.astype(F32)).astype(BF16)
    both = jnp.dot(x_hi, w_ref[...], preferred_element_type=F32)
    lg = both[:, :LANES] + both[:, LANES:] + jnp.dot(x_lo, w_ref[:, :LANES], preferred_element_type=F32)
    lane = lax.broadcasted_iota(jnp.int32, lg.shape, 1)
    is_group = lane < N_GROUPS
    gmax = jnp.max(jnp.where(is_group, lg, -jnp.inf), axis=-1, keepdims=True)
    ge = jnp.where(is_group, jnp.exp(lg - gmax), 0.0)
    gp = ge / jnp.sum(ge, axis=-1, keepdims=True)
    g_top_p = jnp.max(jnp.where(is_group, gp, -1.0), axis=-1, keepdims=True)
    g_top = jnp.min(jnp.where(is_group & (gp == g_top_p), lane, LANES), axis=-1, keepdims=True)
    first = N_GROUPS + g_top * EXPERTS_PER_GROUP
    in_group = (lane >= first) & (lane < first + EXPERTS_PER_GROUP)
    emax = jnp.max(jnp.where(in_group, lg, -jnp.inf), axis=-1, keepdims=True)
    ee = jnp.where(in_group, jnp.exp(lg - emax), 0.0)
    ep = ee / jnp.sum(ee, axis=-1, keepdims=True)
    p1 = jnp.max(jnp.where(in_group, ep, -1.0), axis=-1, keepdims=True)
    i1 = jnp.min(jnp.where(in_group & (ep == p1), lane, LANES), axis=-1, keepdims=True)
    rest = in_group & (lane != i1)
    p2 = jnp.max(jnp.where(rest, ep, -1.0), axis=-1, keepdims=True)
    i2 = jnp.min(jnp.where(rest & (ep == p2), lane, LANES), axis=-1, keepdims=True)
    denom = p1 + p2
    gate1 = g_top_p * p1 / denom
    gate2 = g_top_p * p2 / denom
    e1 = (i1 - N_GROUPS).astype(F32)
    e2 = (i2 - N_GROUPS).astype(F32)

    @pl.when(pl.program_id(0) == 0)
    def _():
        cnt_ref[...] = jnp.zeros_like(cnt_ref)

    chosen = jnp.where((lane == i1) | (lane == i2), 1.0, 0.0)
    cnt_ref[...] = cnt_ref[...] + jnp.sum(chosen, axis=0, keepdims=True)
    meta_ref[...] = jnp.where(lane == 0, e1, jnp.where(lane == 1, e2, jnp.where(lane == 2, gate1,
                              jnp.where(lane == 3, gate2, 0.0))))


def _router(h, gain, w_split, tm=256):
    n, d = h.shape
    tm = min(tm, n)
    return pl.pallas_call(
        _router_kernel,
        grid=(n // tm,),
        in_specs=[pl.BlockSpec((tm, d), lambda i: (i, 0)),
                  pl.BlockSpec((1, d), lambda i: (0, 0)),
                  pl.BlockSpec((d, 2 * LANES), lambda i: (0, 0))],
        out_specs=[pl.BlockSpec((tm, d), lambda i: (i, 0)), pl.BlockSpec((tm, LANES), lambda i: (i, 0)),
                   pl.BlockSpec((1, LANES), lambda i: (0, 0))],
        out_shape=[jax.ShapeDtypeStruct((n, d), F32), jax.ShapeDtypeStruct((n, LANES), F32),
                   jax.ShapeDtypeStruct((1, LANES), F32)],
        compiler_params=_params("arbitrary"),
        name="moe_router",
    )(h, gain.reshape(1, d).astype(F32), w_split)


def _row_copy(src_ref, dst_ref, src_row, dst_row, sem):
    return pltpu.make_async_copy(src_ref.at[pl.ds(src_row, 1)], dst_ref.at[pl.ds(dst_row, 1)], sem)


def _expert_kernel(be_ref, nb_ref, tok_ref, dst_ref, par_ref, nxt_ref,
                   xn_hbm, wg_hbm, wu_hbm, wd_hbm, y_hbm,
                   xbuf, ybuf, xsem, ysem, wg32, wu32, wd32, wsem, wg16, wu16, wd16, *, rows):
    i = pl.program_id(0)
    n_used = nb_ref[0]
    slot = lax.rem(i, RESULT_SLOTS)
    prev = lax.rem(i + RESULT_SLOTS - 1, RESULT_SLOTS)
    xslot = lax.rem(i, GATHER_SLOTS)
    stages = ((wg_hbm, wg32), (wu_hbm, wu32), (wd_hbm, wd32))

    def weights(expert, s):
        return [pltpu.make_async_copy(hbm.at[expert], stage.at[s], wsem.at[s]) for hbm, stage in stages]

    def gather(block, into):
        for r in range(rows):
            _row_copy(xn_hbm, xbuf.at[into], tok_ref[block * rows + r], r, xsem.at[into]).start(ROW_DMA_PRIORITY)

    def scatter(block, frm):
        for r in range(rows):
            _row_copy(ybuf.at[frm], y_hbm, r, dst_ref[block * rows + r], ysem.at[frm]).start(ROW_DMA_PRIORITY)

    def wait_gather(s):
        pltpu.make_async_copy(xn_hbm.at[pl.ds(0, rows)], xbuf.at[s], xsem.at[s]).wait()

    def wait_scatter(s):
        pltpu.make_async_copy(ybuf.at[s], y_hbm.at[pl.ds(0, rows)], ysem.at[s]).wait()

    def ahead(k):
        return jnp.minimum(i + k, n_used - 1)

    def mlp():
        x = xbuf[xslot].astype(BF16)
        g = jnp.dot(x, wg16[...], preferred_element_type=F32)
        u = jnp.dot(x, wu16[...], preferred_element_type=F32)
        hmid = (g * _sigmoid(g)) * u
        ybuf[slot] = jnp.dot(hmid.astype(BF16), wd16[...], preferred_element_type=F32)

    @pl.when(i == 0)
    def _():
        for c in weights(be_ref[0], 0):
            c.start(WEIGHT_DMA_PRIORITY)
        for k in range(GATHER_SLOTS - 1):
            gather(ahead(k), k)
        ybuf[1] = jnp.zeros((rows, ybuf.shape[2]), F32)
        n_real = y_hbm.shape[0] - RESULT_SLOTS * rows
        clears = [pltpu.make_async_copy(ybuf.at[1], y_hbm.at[pl.ds(n_real + p * rows, rows)], ysem.at[1])
                  for p in range(RESULT_SLOTS)]
        for c in clears:
            c.start()
        for c in clears:
            c.wait()

    new_expert = jnp.logical_or(i == 0, be_ref[i] != be_ref[jnp.maximum(i - 1, 0)])

    @pl.when(jnp.logical_and(i < n_used, new_expert))
    def _():
        s = par_ref[i]
        for c in weights(0, s):
            c.wait()

        @pl.when(nxt_ref[i] >= 0)
        def _():
            for c in weights(nxt_ref[i], 1 - s):
                c.start(WEIGHT_DMA_PRIORITY)

        wg16[...] = wg32[s].astype(BF16)
        wu16[...] = wu32[s].astype(BF16)
        wd16[...] = wd32[s].astype(BF16)

    @pl.when(i == 0)
    def _():
        wait_gather(xslot)
        gather(ahead(GATHER_SLOTS - 1), lax.rem(i + GATHER_SLOTS - 1, GATHER_SLOTS))
        mlp()

    @pl.when(jnp.logical_and(i >= 1, i < n_used))
    def _():
        @pl.when(i >= RESULT_SLOTS)
        def _():
            wait_scatter(slot)

        wait_gather(xslot)
        gather(ahead(GATHER_SLOTS - 1), lax.rem(i + GATHER_SLOTS - 1, GATHER_SLOTS))
        scatter(i - 1, prev)
        mlp()

    @pl.when(i == n_used)
    def _():
        scatter(i - 1, prev)
        for k in range(1, RESULT_SLOTS + 1):
            @pl.when(i >= k)
            def _():
                wait_scatter(lax.rem(i + RESULT_SLOTS - k, RESULT_SLOTS))

        for k in range(GATHER_SLOTS - 1):
            wait_gather(lax.rem(i + k, GATHER_SLOTS))


def _expert_mlp(xn, row_tok, row_dst, block_expert, n_used, parity, next_expert, w_gate, w_up, w_down, n_out,
                rows=EXPERT_ROWS):
    d = xn.shape[1]
    n_rows = row_tok.shape[0]
    ff = w_gate.shape[-1]
    any_spec = pl.BlockSpec(memory_space=pl.ANY)
    return pl.pallas_call(
        functools.partial(_expert_kernel, rows=rows),
        grid_spec=pltpu.PrefetchScalarGridSpec(
            num_scalar_prefetch=6,
            grid=(n_rows // rows,),
            in_specs=[any_spec, any_spec, any_spec, any_spec],
            out_specs=any_spec,
            scratch_shapes=[pltpu.VMEM((GATHER_SLOTS, rows, d), F32), pltpu.VMEM((RESULT_SLOTS, rows, d), F32),
                            pltpu.SemaphoreType.DMA((GATHER_SLOTS,)), pltpu.SemaphoreType.DMA((RESULT_SLOTS,)),
                            pltpu.VMEM((2, d, ff), F32), pltpu.VMEM((2, d, ff), F32), pltpu.VMEM((2, ff, d), F32),
                            pltpu.SemaphoreType.DMA((2,)),
                            pltpu.VMEM((d, ff), BF16), pltpu.VMEM((d, ff), BF16), pltpu.VMEM((ff, d), BF16)]),
        out_shape=jax.ShapeDtypeStruct((n_out, d), F32),
        compiler_params=_params("arbitrary"),
        name="moe_experts",
    )(block_expert, n_used, row_tok, row_dst, parity, next_expert, xn, w_gate, w_up, w_down)


def _combine_kernel(*refs, with_norm):
    if with_norm:
        h_ref, meta_ref, y0_ref, y1_ref, g_ref, o_ref, hn_ref = refs
    else:
        h_ref, meta_ref, y0_ref, y1_ref, o_ref = refs
    meta = meta_ref[...]
    out = h_ref[...] + (y0_ref[...] * meta[:, 2:3] + y1_ref[...] * meta[:, 3:4])
    o_ref[...] = out
    if with_norm:
        ms = jnp.mean(out * out, axis=-1, keepdims=True)
        hn_ref[...] = (out * lax.rsqrt(ms + NORM_EPS) * g_ref[...]).astype(hn_ref.dtype)


def _combine(h, meta, y2, next_gain=None, tm=256):
    n, d = h.shape
    tm = min(tm, n)
    with_norm = next_gain is not None
    row_spec = pl.BlockSpec((tm, d), lambda i: (i, 0))
    in_specs = [row_spec, pl.BlockSpec((tm, LANES), lambda i: (i, 0)), row_spec,
                pl.BlockSpec((tm, d), lambda i: (n // tm + i, 0))]
    args = [h, meta, y2, y2]
    out_specs, out_shape = [row_spec], [jax.ShapeDtypeStruct((n, d), F32)]
    if with_norm:
        in_specs.append(pl.BlockSpec((1, d), lambda i: (0, 0)))
        args.append(next_gain.reshape(1, d).astype(F32))
        out_specs.append(row_spec)
        out_shape.append(jax.ShapeDtypeStruct((n, d), BF16))
    outs = pl.pallas_call(
        functools.partial(_combine_kernel, with_norm=with_norm),
        grid=(n // tm,),
        in_specs=in_specs,
        out_specs=out_specs,
        out_shape=out_shape,
        compiler_params=_params("parallel"),
        name="moe_combine",
    )(*args)
    return (outs[0], outs[1]) if with_norm else (outs[0], None)


def _split_router_weights(w_group, w_expert):
    d = w_group.shape[0]
    w = jnp.concatenate([w_group.astype(F32), w_expert.astype(F32),
                         jnp.zeros((d, LANES - N_GROUPS - N_EXPERTS), F32)], axis=1)
    hi = w.astype(BF16)
    lo = (w - hi.astype(F32)).astype(BF16)
    return jnp.concatenate([hi, lo], axis=1)


def _hier_moe(h, ffn_gain, w_group, w_expert, w_gate, w_up, w_down, layer, next_gain=None):
    n, d = h.shape
    rows = EXPERT_ROWS
    xn, meta, cnt = _router(h, ffn_gain, _split_router_weights(w_group, w_expert))
    ids = jnp.arange(N_EXPERTS, dtype=jnp.int32)
    expert = meta[:, :TOP_K].astype(jnp.int32).reshape(-1)
    counts = cnt[0, N_GROUPS:N_GROUPS + N_EXPERTS].astype(jnp.int32)
    n_assign = n * TOP_K
    n_blocks = n_assign // rows + N_EXPERTS
    n_rows = n_blocks * rows
    order = jnp.argsort(expert, stable=True).astype(jnp.int32)
    start = jnp.cumsum(counts) - counts
    pad_counts = (counts + rows - 1) // rows * rows
    pad_end = jnp.cumsum(pad_counts)
    pad_start = pad_end - pad_counts
    n_used = (pad_end[-1] // rows).astype(jnp.int32).reshape(1)
    blk = jnp.minimum(jnp.arange(n_blocks, dtype=jnp.int32), n_used[0] - 1)
    block_expert = jnp.sum((pad_end[None, :] <= (blk * rows)[:, None]).astype(jnp.int32), axis=1)
    block_expert = jnp.minimum(block_expert, N_EXPERTS - 1)
    b_idx = jnp.arange(n_blocks, dtype=jnp.int32)
    first = b_idx * rows - pad_start[block_expert]
    src = jnp.clip(start[block_expert] + first, 0, n_assign)
    j = jnp.arange(rows, dtype=jnp.int32)[None, :]
    order_padded = jnp.concatenate([order, jnp.zeros((rows,), jnp.int32)])
    choice = order_padded[src[:, None] + j]
    real = (b_idx < n_used[0])[:, None] & (first[:, None] + j < counts[block_expert][:, None])
    spare = TOP_K * n + (b_idx % RESULT_SLOTS)[:, None] * rows + j
    row_tok = jnp.where(real, choice // TOP_K, 0).reshape(-1)
    row_dst = jnp.where(real, (choice % TOP_K) * n + choice // TOP_K, spare).reshape(-1)
    nonempty = counts > 0
    position = jnp.cumsum(nonempty.astype(jnp.int32)) - 1
    later = jnp.where(nonempty[None, :] & (ids[None, :] > ids[:, None]), ids[None, :], N_EXPERTS)
    following = jnp.min(later, axis=1)
    following = jnp.where(following < N_EXPERTS, following + layer * N_EXPERTS, -1)
    y2 = _expert_mlp(xn, row_tok, row_dst, block_expert + layer * N_EXPERTS, n_used,
                     position[block_expert] % 2, following[block_expert], w_gate, w_up, w_down,
                     TOP_K * n + RESULT_SLOTS * rows, rows)
    return _combine(h, meta, y2, next_gain)


def kernel(x, positions, mix_norm, ffn_norm, hgrn_w_in, hgrn_out_gain, hgrn_w_out, hgrn_lb_logits, diff_w_in, diff_q_gain, diff_k_gain, diff_lambda, diff_sub_gain, diff_w_out, moe_w_group, moe_w_expert, moe_w_gate, moe_w_up, moe_w_down):
    b, s, d = x.shape
    n = b * s
    depth = mix_norm.shape[0]
    lower_bounds = jnp.cumsum(jax.nn.softmax(hgrn_lb_logits.astype(F32), axis=0), axis=0)
    h = x.reshape(n, d)
    w_gate = moe_w_gate.reshape((-1,) + moe_w_gate.shape[2:])
    w_up = moe_w_up.reshape((-1,) + moe_w_up.shape[2:])
    w_down = moe_w_down.reshape((-1,) + moe_w_down.shape[2:])
    hn = _rmsnorm_bf16(h, mix_norm[0])
    for layer in range(depth):
        j = layer // N_MIXERS
        if layer % N_MIXERS == 0:
            w_heads = (hgrn_w_in[j].reshape(d, 4, HGRN_HEADS, HGRN_DIM).transpose(2, 0, 1, 3)
                       .reshape(HGRN_HEADS, d, 4 * HGRN_DIM).astype(BF16))
            o = _hgrn_mixer(hn.reshape(b, s, d), w_heads, lower_bounds[layer], hgrn_out_gain[j])
            h = _matmul_residual(o.reshape(n, d), hgrn_w_out[j].astype(BF16), h)
        else:
            lam_init = 0.8 - 0.6 * math.exp(-0.3 * layer)
            hd = DIFF_HEAD_DIM
            half = hd // 2
            inv_freq = ROPE_THETA ** (-jnp.arange(half, dtype=F32) / half)
            ang = positions.astype(F32).reshape(n, 1) * inv_freq[None, :]
            cosf = jnp.concatenate([jnp.cos(ang), jnp.cos(ang)], axis=1)
            sins = jnp.concatenate([-jnp.sin(ang), jnp.sin(ang)], axis=1)
            n_heads = 2 * DIFF_HEADS
            n_qk = 2 * n_heads * hd
            gains = jnp.concatenate([jnp.tile(diff_q_gain[j].astype(F32), n_heads) * (hd ** -0.5),
                                     jnp.tile(diff_k_gain[j].astype(F32), n_heads)]).reshape(1, -1)
            w_in = diff_w_in[j].astype(BF16)
            qk = _qk_proj(hn, w_in[:, :n_qk], gains, cosf, sins)
            v = _matmul_bf16(hn, w_in[:, n_qk:])
            o = _diff_attention(qk.reshape(b, s, -1), v.reshape(b, s, -1), diff_lambda[j], diff_sub_gain[j],
                                lam_init)
            h = _matmul_residual(o.reshape(n, d), diff_w_out[j].astype(BF16), h)
        next_gain = mix_norm[layer + 1] if layer + 1 < depth else None
        h, hn = _hier_moe(h, ffn_norm[layer], moe_w_group[layer], moe_w_expert[layer], w_gate, w_up, w_down, layer,
                          next_gain)
    return h.reshape(b, s, d)
```

```python
import functools
import math

import jax
import jax.numpy as jnp
import numpy as np
from jax import lax
from jax.experimental import pallas as pl
from jax.experimental.pallas import tpu as pltpu

F32 = jnp.float32
BF16 = jnp.bfloat16

NORM_EPS = 1e-6
N_MIXERS = 2
HGRN_HEADS = 16
HGRN_DIM = 128
HGRN_BLOCK = 128
HGRN_CHUNK = 512
HGRN_SUB = 16
DIFF_HEADS = 8
DIFF_HEAD_DIM = 128
ROPE_THETA = 10000.0
ATTN_TILE = 256
N_GROUPS = 8
EXPERTS_PER_GROUP = 8
N_EXPERTS = N_GROUPS * EXPERTS_PER_GROUP
TOP_K = 2
EXPERT_ROWS = 256
GATHER_SLOTS = 3
RESULT_SLOTS = 3
ROW_DMA_PRIORITY = 0
WEIGHT_DMA_PRIORITY = 1
LANES = 128
VMEM_LIMIT_BYTES = 56 * 1024 * 1024


def _params(*semantics):
    return pltpu.CompilerParams(dimension_semantics=semantics, vmem_limit_bytes=VMEM_LIMIT_BYTES)


def _norm_kernel(h_ref, g_ref, o_ref):
    x = h_ref[...]
    ms = jnp.mean(x * x, axis=-1, keepdims=True)
    o_ref[...] = (x * lax.rsqrt(ms + NORM_EPS) * g_ref[...]).astype(o_ref.dtype)


def _rmsnorm_bf16(h, gain, tm=512):
    n, d = h.shape
    return pl.pallas_call(
        _norm_kernel,
        grid=(n // tm,),
        in_specs=[pl.BlockSpec((tm, d), lambda i: (i, 0)), pl.BlockSpec((1, d), lambda i: (0, 0))],
        out_specs=pl.BlockSpec((tm, d), lambda i: (i, 0)),
        out_shape=jax.ShapeDtypeStruct((n, d), BF16),
        compiler_params=_params("parallel"),
        name="rmsnorm_bf16",
    )(h, gain.reshape(1, d).astype(F32))


def _mm_res_kernel(a_ref, w_ref, r_ref, o_ref):
    o_ref[...] = r_ref[...] + jnp.dot(a_ref[...], w_ref[...], preferred_element_type=F32)


def _matmul_residual(a, w, res, tm=1024, tn=1024):
    n, k = a.shape
    nout = w.shape[1]
    tm, tn = min(tm, n), min(tn, nout)
    return pl.pallas_call(
        _mm_res_kernel,
        grid=(n // tm, nout // tn),
        in_specs=[pl.BlockSpec((tm, k), lambda i, j: (i, 0)),
                  pl.BlockSpec((k, tn), lambda i, j: (0, j)),
                  pl.BlockSpec((tm, tn), lambda i, j: (i, j))],
        out_specs=pl.BlockSpec((tm, tn), lambda i, j: (i, j)),
        out_shape=jax.ShapeDtypeStruct((n, nout), F32),
        compiler_params=_params("parallel", "parallel"),
        name="matmul_residual",
    )(a, w, res)


def _sigmoid(x):
    return 1.0 / (1.0 + jnp.exp(-x))


def _ref_rows(a_ref, half, block):
    span = 2 * half
    parts = [jnp.broadcast_to(a_ref[g * span + half - 1:g * span + half, :], (span, a_ref.shape[1]))
             for g in range(block // span)]
    return parts[0] if len(parts) == 1 else jnp.concatenate(parts, axis=0)


def _cumsum_rows(x, row):
    shift = 1
    while shift < x.shape[0]:
        x = x + jnp.where(row >= shift, pltpu.roll(x, shift, 0), 0.0)
        shift *= 2
    return x


def _hgrn_kernel(hn_ref, w_ref, lb_ref, og_ref, o_ref, proj_a, proj_b, cum_a, cum_b, *, seq, block, chunk):
    dim = HGRN_DIM
    blocks_per_chunk = chunk // block
    n_chunks = seq // chunk
    lb = lb_ref[0]
    og = og_ref[...]
    t_idx = lax.broadcasted_iota(jnp.int32, (block, block), 0)
    s_idx = lax.broadcasted_iota(jnp.int32, (block, block), 1)
    causal = s_idx <= t_idx
    split = t_idx ^ s_idx
    row = lax.broadcasted_iota(jnp.int32, (block, dim), 0)
    halves = []
    half = block // 2
    while half >= HGRN_SUB:
        halves.append(half)
        half //= 2
    contract_last = (((1,), (1,)), ((), ()))
    contract_first = (((0,), (0,)), ((), ()))

    def project(c, buf):
        r0 = pl.multiple_of(c * chunk, chunk)
        buf[...] = jnp.dot(hn_ref[0, pl.ds(r0, chunk), :], w_ref[0], preferred_element_type=F32)

    def one_block(buf, cum_ref, j, r0, state_t):
        rows = slice(j * block, (j + 1) * block)
        pq = buf[rows, 0 * dim:1 * dim]
        pf = buf[rows, 1 * dim:2 * dim]
        v16 = buf[rows, 2 * dim:3 * dim].astype(BF16)
        pg = buf[rows, 3 * dim:4 * dim]
        q = pq * _sigmoid(pq)
        forget = lb + (1.0 - lb) * _sigmoid(pf)
        k = 1.0 - forget
        cum = _cumsum_rows(jnp.log(forget), row)
        cum_ref[j] = cum
        blk = cum_ref.at[j]
        total = blk[block - 1:block, :]
        scores = None
        for half in halves:
            e = jnp.exp(-jnp.abs(cum - _ref_rows(blk, half, block)))
            p = lax.dot_general((q * e).astype(BF16), (k * e).astype(BF16), contract_last,
                                preferred_element_type=F32)
            scores = p if scores is None else jnp.where(split >= 2 * half, scores, p)
        sub_ref = jnp.concatenate(
            [jnp.zeros((HGRN_SUB, dim), F32)]
            + [jnp.broadcast_to(blk[s * HGRN_SUB - 1:s * HGRN_SUB, :], (HGRN_SUB, dim))
               for s in range(1, block // HGRN_SUB)], axis=0)
        loc = cum - sub_ref
        p = lax.dot_general((q * jnp.exp(loc)).astype(BF16), (k * jnp.exp(-loc)).astype(BF16),
                            contract_last, preferred_element_type=F32)
        scores = p if scores is None else jnp.where(split >= HGRN_SUB, scores, p)
        scores = jnp.where(causal, scores, 0.0)
        o = jnp.dot(scores.astype(BF16), v16, preferred_element_type=F32)
        o = o + lax.dot_general((q * jnp.exp(cum)).astype(BF16), state_t.astype(BF16), contract_last,
                                preferred_element_type=F32)
        k_end = (k * jnp.exp(total - cum)).astype(BF16)
        state_t = state_t * jnp.exp(total) + lax.dot_general(v16, k_end, contract_first,
                                                              preferred_element_type=F32)
        ms = jnp.mean(o * o, axis=-1, keepdims=True)
        y = o * lax.rsqrt(ms + NORM_EPS) * og
        o_ref[0, pl.ds(r0 + j * block, block), :] = (y * (pg * _sigmoid(pg))).astype(o_ref.dtype)
        return state_t

    bufs = (proj_a, proj_b)
    cum_refs = (cum_a, cum_b)
    project(0, bufs[0])

    def chunk_pair(cp, state_t):
        for parity in range(2):
            c = 2 * cp + parity
            project(jnp.minimum(c + 1, n_chunks - 1), bufs[1 - parity])
            r0 = pl.multiple_of(c * chunk, chunk)
            for j in range(blocks_per_chunk):
                state_t = one_block(bufs[parity], cum_refs[parity], j, r0, state_t)
        return state_t

    state = lax.fori_loop(0, n_chunks // 2, chunk_pair, jnp.zeros((dim, dim), F32))
    if n_chunks % 2:
        r0 = (n_chunks - 1) * chunk
        for j in range(blocks_per_chunk):
            state = one_block(bufs[0], cum_refs[0], j, r0, state)


def _hgrn_mixer(hn, w_heads, lb, out_gain):
    b, s, d = hn.shape
    dim = HGRN_DIM
    block = min(HGRN_BLOCK, s)
    chunk = min(HGRN_CHUNK, s)
    return pl.pallas_call(
        functools.partial(_hgrn_kernel, seq=s, block=block, chunk=chunk),
        grid=(b, HGRN_HEADS),
        in_specs=[pl.BlockSpec((1, s, d), lambda i, h: (i, 0, 0)),
                  pl.BlockSpec((1, d, 4 * dim), lambda i, h: (h, 0, 0)),
                  pl.BlockSpec((1, 1, dim), lambda i, h: (h, 0, 0)),
                  pl.BlockSpec((1, dim), lambda i, h: (0, 0))],
        out_specs=pl.BlockSpec((1, s, dim), lambda i, h: (i, 0, h)),
        out_shape=jax.ShapeDtypeStruct((b, s, d), BF16),
        scratch_shapes=[pltpu.VMEM((chunk, 4 * dim), F32), pltpu.VMEM((chunk, 4 * dim), F32),
                        pltpu.VMEM((chunk // block, block, dim), F32),
                        pltpu.VMEM((chunk // block, block, dim), F32)],
        compiler_params=_params("parallel", "arbitrary"),
        name="hgrn_mixer",
    )(hn, w_heads, lb.reshape(HGRN_HEADS, 1, dim), out_gain.reshape(1, dim).astype(F32))


def _qk_kernel(a_ref, w_ref, g_ref, cos_ref, sin_ref, o_ref, *, tn):
    hd = DIFF_HEAD_DIM
    a = a_ref[...]
    cosf = cos_ref[...]
    sins = sin_ref[...]
    starts = list(range(0, tn, 2 * hd))
    nxt = jnp.dot(a, w_ref[:, 0:2 * hd], preferred_element_type=F32)
    for n_chunk, c0 in enumerate(starts):
        acc = nxt
        if n_chunk + 1 < len(starts):
            c1 = starts[n_chunk + 1]
            nxt = jnp.dot(a, w_ref[:, c1:c1 + 2 * hd], preferred_element_type=F32)
        for hh in range(2):
            x = acc[:, hh * hd:(hh + 1) * hd]
            sl = slice(c0 + hh * hd, c0 + (hh + 1) * hd)
            ms = jnp.mean(x * x, axis=-1, keepdims=True)
            y = x * lax.rsqrt(ms + NORM_EPS) * g_ref[:, sl]
            rot = pltpu.roll(y, hd // 2, 1)
            o_ref[:, sl] = (y * cosf + rot * sins).astype(o_ref.dtype)


def _qk_proj(hn, w, gains, cosf, sins, tm=256, tn=1024):
    n, d = hn.shape
    nout = w.shape[1]
    tm = min(tm, n)
    return pl.pallas_call(
        functools.partial(_qk_kernel, tn=tn),
        grid=(nout // tn, n // tm),
        in_specs=[pl.BlockSpec((tm, d), lambda j, i: (i, 0)),
                  pl.BlockSpec((d, tn), lambda j, i: (0, j)),
                  pl.BlockSpec((1, tn), lambda j, i: (0, j)),
                  pl.BlockSpec((tm, DIFF_HEAD_DIM), lambda j, i: (i, 0)),
                  pl.BlockSpec((tm, DIFF_HEAD_DIM), lambda j, i: (i, 0))],
        out_specs=pl.BlockSpec((tm, tn), lambda j, i: (i, j)),
        out_shape=jax.ShapeDtypeStruct((n, nout), BF16),
        compiler_params=_params("parallel", "arbitrary"),
        name="qk_proj",
    )(hn, w, gains, cosf, sins)


def _mm_bf16_kernel(a_ref, w_ref, o_ref):
    o_ref[...] = jnp.dot(a_ref[...], w_ref[...], preferred_element_type=F32).astype(o_ref.dtype)


def _matmul_bf16(a, w, tm=1024, tn=1024):
    n, k = a.shape
    nout = w.shape[1]
    tm, tn = min(tm, n), min(tn, nout)
    return pl.pallas_call(
        _mm_bf16_kernel,
        grid=(n // tm, nout // tn),
        in_specs=[pl.BlockSpec((tm, k), lambda i, j: (i, 0)), pl.BlockSpec((k, tn), lambda i, j: (0, j))],
        out_specs=pl.BlockSpec((tm, tn), lambda i, j: (i, j)),
        out_shape=jax.ShapeDtypeStruct((n, nout), BF16),
        compiler_params=_params("parallel", "parallel"),
        name="matmul_bf16",
    )(a, w)


def _attn_kernel(q_ref, k_ref, v_ref, lam_ref, sg_ref, o_ref, kt_ref, s_ref, m_ref, l_ref, acc_ref, *, seq, tile,
                 lam_init):
    hd = DIFF_HEAD_DIM
    lf = lam_ref[...]
    lam = (jnp.exp(jnp.sum(lf[0:1] * lf[1:2], axis=-1, keepdims=True))
           - jnp.exp(jnp.sum(lf[2:3] * lf[3:4], axis=-1, keepdims=True)) + lam_init)
    offset = {w: (lax.broadcasted_iota(jnp.int32, (tile, w), 1) - lax.broadcasted_iota(jnp.int32, (tile, w), 0))
              for w in (tile, 2 * tile)}

    for c in range(2):
        for t in range(seq // tile):
            kt = k_ref[0, t * tile:(t + 1) * tile, c * hd:(c + 1) * hd].astype(F32).T
            kt_ref[c, :, t * tile:(t + 1) * tile] = kt.astype(BF16)

    def lane_fold(x, op):
        out = x[:, :LANES]
        for t in range(1, x.shape[1] // LANES):
            out = op(out, x[:, t * LANES:(t + 1) * LANES])
        return out

    def qblock(i, carry):
        q0 = pl.multiple_of(i * tile, tile)
        odd = lax.rem(i, 2)

        def score(k0, width):
            visible = offset[width] <= q0 - k0
            for c in range(2):
                q = q_ref[0, pl.ds(q0, tile), c * hd:(c + 1) * hd]
                s = jnp.dot(q, kt_ref[c, :, pl.ds(k0, width)], preferred_element_type=F32)
                s = jnp.where(visible, s, -jnp.inf)
                s_ref[c, :, pl.ds(k0, width)] = s
                m_ref[c] = jnp.maximum(m_ref[c], lane_fold(s, jnp.maximum))

        def value(k0, width):
            vj = v_ref[0, pl.ds(k0, width), :]
            for c in range(2):
                m = m_ref[c]
                e = jnp.exp(s_ref[c, :, pl.ds(k0, width)] - jnp.concatenate([m] * (width // LANES), axis=1))
                l_ref[c] = l_ref[c] + lane_fold(e, jnp.add)
                acc_ref[c] = acc_ref[c] + jnp.dot(e.astype(BF16), vj, preferred_element_type=F32)

        for c in range(2):
            m_ref[c] = jnp.full((tile, LANES), -jnp.inf, F32)

        @pl.when(odd == 0)
        def _():
            score(0, tile)

        def score_pair(p, c2):
            score(pl.multiple_of((1 - odd) * tile + p * 2 * tile, tile), 2 * tile)
            return c2

        lax.fori_loop(0, (i + 1) // 2, score_pair, 0)
        for c in range(2):
            m_ref[c] = jnp.broadcast_to(jnp.max(m_ref[c], axis=-1, keepdims=True), (tile, LANES))
            l_ref[c] = jnp.zeros((tile, LANES), F32)
            acc_ref[c] = jnp.zeros((tile, 2 * hd), F32)

        @pl.when(odd == 0)
        def _():
            value(0, tile)

        def value_pair(p, c2):
            value(pl.multiple_of((1 - odd) * tile + p * 2 * tile, tile), 2 * tile)
            return c2

        lax.fori_loop(0, (i + 1) // 2, value_pair, 0)
        inv = [1.0 / jnp.sum(l_ref[c], axis=-1, keepdims=True) for c in range(2)]
        o = acc_ref[0] * inv[0] - lam * (acc_ref[1] * inv[1])
        ms = jnp.mean(o * o, axis=-1, keepdims=True)
        y = o * lax.rsqrt(ms + NORM_EPS) * sg_ref[...]
        o_ref[0, pl.ds(q0, tile), :] = (y * (1.0 - lam_init)).astype(o_ref.dtype)
        return carry

    lax.fori_loop(0, seq // tile, qblock, 0)


def _diff_attention(qk, v, lam, sub_gain, lam_init):
    b, s, _ = qk.shape
    hd = DIFF_HEAD_DIM
    tile = min(ATTN_TILE, s)
    nh = DIFF_HEADS
    return pl.pallas_call(
        functools.partial(_attn_kernel, seq=s, tile=tile, lam_init=lam_init),
        grid=(b, nh),
        in_specs=[pl.BlockSpec((1, s, 2 * hd), lambda i, h: (i, 0, h)),
                  pl.BlockSpec((1, s, 2 * hd), lambda i, h: (i, 0, nh + h)),
                  pl.BlockSpec((1, s, 2 * hd), lambda i, h: (i, 0, h)),
                  pl.BlockSpec((4, hd), lambda i, h: (0, 0)),
                  pl.BlockSpec((1, 2 * hd), lambda i, h: (0, 0))],
        out_specs=pl.BlockSpec((1, s, 2 * hd), lambda i, h: (i, 0, h)),
        out_shape=jax.ShapeDtypeStruct((b, s, nh * 2 * hd), BF16),
        scratch_shapes=[pltpu.VMEM((2, hd, s), BF16), pltpu.VMEM((2, tile, s), F32), pltpu.VMEM((2, tile, LANES), F32),
                        pltpu.VMEM((2, tile, LANES), F32), pltpu.VMEM((2, tile, 2 * hd), F32)],
        compiler_params=_params("parallel", "parallel"),
        name="diff_attention",
    )(qk, qk, v, lam.astype(F32), sub_gain.reshape(1, 2 * hd).astype(F32))


def _router_kernel(h_ref, g_ref, w_ref, xn_ref, meta_ref, cnt_ref):
    x = h_ref[...]
    ms = jnp.mean(x * x, axis=-1, keepdims=True)
    xn = x * lax.rsqrt(ms + NORM_EPS) * g_ref[...]
    xn_ref[...] = xn
    x_hi = xn.astype(BF16)
    x_lo = (xn - x_hi.astype(F32)).astype(BF16)
    both = jnp.dot(x_hi, w_ref[...], preferred_element_type=F32)
    lg = both[:, :LANES] + both[:, LANES:] + jnp.dot(x_lo, w_ref[:, :LANES], preferred_element_type=F32)
    lane = lax.broadcasted_iota(jnp.int32, lg.shape, 1)
    is_group = lane < N_GROUPS
    gmax = jnp.max(jnp.where(is_group, lg, -jnp.inf), axis=-1, keepdims=True)
    ge = jnp.where(is_group, jnp.exp(lg - gmax), 0.0)
    gp = ge / jnp.sum(ge, axis=-1, keepdims=True)
    g_top_p = jnp.max(jnp.where(is_group, gp, -1.0), axis=-1, keepdims=True)
    g_top = jnp.min(jnp.where(is_group & (gp == g_top_p), lane, LANES), axis=-1, keepdims=True)
    first = N_GROUPS + g_top * EXPERTS_PER_GROUP
    in_group = (lane >= first) & (lane < first + EXPERTS_PER_GROUP)
    emax = jnp.max(jnp.where(in_group, lg, -jnp.inf), axis=-1, keepdims=True)
    ee = jnp.where(in_group, jnp.exp(lg - emax), 0.0)
    ep = ee / jnp.sum(ee, axis=-1, keepdims=True)
    p1 = jnp.max(jnp.where(in_group, ep, -1.0), axis=-1, keepdims=True)
    i1 = jnp.min(jnp.where(in_group & (ep == p1), lane, LANES), axis=-1, keepdims=True)
    rest = in_group & (lane != i1)
    p2 = jnp.max(jnp.where(rest, ep, -1.0), axis=-1, keepdims=True)
    i2 = jnp.min(jnp.where(rest & (ep == p2), lane, LANES), axis=-1, keepdims=True)
    denom = p1 + p2
    gate1 = g_top_p * p1 / denom
    gate2 = g_top_p * p2 / denom
    e1 = (i1 - N_GROUPS).astype(F32)
    e2 = (i2 - N_GROUPS).astype(F32)

    @pl.when(pl.program_id(0) == 0)
    def _():
        cnt_ref[...] = jnp.zeros_like(cnt_ref)

    chosen = jnp.where((lane == i1) | (lane == i2), 1.0, 0.0)
    cnt_ref[...] = cnt_ref[...] + jnp.sum(chosen, axis=0, keepdims=True)
    meta_ref[...] = jnp.where(lane == 0, e1, jnp.where(lane == 1, e2, jnp.where(lane == 2, gate1,
                              jnp.where(lane == 3, gate2, 0.0))))


def _router(h, gain, w_split, tm=256):
    n, d = h.shape
    tm = min(tm, n)
    return pl.pallas_call(
        _router_kernel,
        grid=(n // tm,),
        in_specs=[pl.BlockSpec((tm, d), lambda i: (i, 0)),
                  pl.BlockSpec((1, d), lambda i: (0, 0)),
                  pl.BlockSpec((d, 2 * LANES), lambda i: (0, 0))],
        out_specs=[pl.BlockSpec((tm, d), lambda i: (i, 0)), pl.BlockSpec((tm, LANES), lambda i: (i, 0)),
                   pl.BlockSpec((1, LANES), lambda i: (0, 0))],
        out_shape=[jax.ShapeDtypeStruct((n, d), F32), jax.ShapeDtypeStruct((n, LANES), F32),
                   jax.ShapeDtypeStruct((1, LANES), F32)],
        compiler_params=_params("arbitrary"),
        name="moe_router",
    )(h, gain.reshape(1, d).astype(F32), w_split)


def _row_copy(src_ref, dst_ref, src_row, dst_row, sem):
    return pltpu.make_async_copy(src_ref.at[pl.ds(src_row, 1)], dst_ref.at[pl.ds(dst_row, 1)], sem)


def _expert_kernel(be_ref, nb_ref, tok_ref, dst_ref, par_ref, nxt_ref,
                   xn_hbm, wg_hbm, wu_hbm, wd_hbm, y_hbm,
                   xbuf, ybuf, xsem, ysem, wg32, wu32, wd32, wsem, wg16, wu16, wd16, *, rows):
    i = pl.program_id(0)
    n_used = nb_ref[0]
    slot = lax.rem(i, RESULT_SLOTS)
    prev = lax.rem(i + RESULT_SLOTS - 1, RESULT_SLOTS)
    xslot = lax.rem(i, GATHER_SLOTS)
    stages = ((wg_hbm, wg32), (wu_hbm, wu32), (wd_hbm, wd32))

    def weights(expert, s):
        return [pltpu.make_async_copy(hbm.at[expert], stage.at[s], wsem.at[s]) for hbm, stage in stages]

    def gather(block, into):
        for r in range(rows):
            _row_copy(xn_hbm, xbuf.at[into], tok_ref[block * rows + r], r, xsem.at[into]).start(ROW_DMA_PRIORITY)

    def scatter(block, frm):
        for r in range(rows):
            _row_copy(ybuf.at[frm], y_hbm, r, dst_ref[block * rows + r], ysem.at[frm]).start(ROW_DMA_PRIORITY)

    def wait_gather(s):
        pltpu.make_async_copy(xn_hbm.at[pl.ds(0, rows)], xbuf.at[s], xsem.at[s]).wait()

    def wait_scatter(s):
        pltpu.make_async_copy(ybuf.at[s], y_hbm.at[pl.ds(0, rows)], ysem.at[s]).wait()

    def ahead(k):
        return jnp.minimum(i + k, n_used - 1)

    def mlp():
        x = xbuf[xslot].astype(BF16)
        g = jnp.dot(x, wg16[...], preferred_element_type=F32)
        u = jnp.dot(x, wu16[...], preferred_element_type=F32)
        hmid = (g * _sigmoid(g)) * u
        ybuf[slot] = jnp.dot(hmid.astype(BF16), wd16[...], preferred_element_type=F32)

    @pl.when(i == 0)
    def _():
        for c in weights(be_ref[0], 0):
            c.start(WEIGHT_DMA_PRIORITY)
        for k in range(GATHER_SLOTS - 1):
            gather(ahead(k), k)
        ybuf[1] = jnp.zeros((rows, ybuf.shape[2]), F32)
        n_real = y_hbm.shape[0] - RESULT_SLOTS * rows
        clears = [pltpu.make_async_copy(ybuf.at[1], y_hbm.at[pl.ds(n_real + p * rows, rows)], ysem.at[1])
                  for p in range(RESULT_SLOTS)]
        for c in clears:
            c.start()
        for c in clears:
            c.wait()

    new_expert = jnp.logical_or(i == 0, be_ref[i] != be_ref[jnp.maximum(i - 1, 0)])

    @pl.when(jnp.logical_and(i < n_used, new_expert))
    def _():
        s = par_ref[i]
        for c in weights(0, s):
            c.wait()

        @pl.when(nxt_ref[i] >= 0)
        def _():
            for c in weights(nxt_ref[i], 1 - s):
                c.start(WEIGHT_DMA_PRIORITY)

        wg16[...] = wg32[s].astype(BF16)
        wu16[...] = wu32[s].astype(BF16)
        wd16[...] = wd32[s].astype(BF16)

    @pl.when(i == 0)
    def _():
        wait_gather(xslot)
        gather(ahead(GATHER_SLOTS - 1), lax.rem(i + GATHER_SLOTS - 1, GATHER_SLOTS))
        mlp()

    @pl.when(jnp.logical_and(i >= 1, i < n_used))
    def _():
        @pl.when(i >= RESULT_SLOTS)
        def _():
            wait_scatter(slot)

        wait_gather(xslot)
        gather(ahead(GATHER_SLOTS - 1), lax.rem(i + GATHER_SLOTS - 1, GATHER_SLOTS))
        scatter(i - 1, prev)
        mlp()

    @pl.when(i == n_used)
    def _():
        scatter(i - 1, prev)
        for k in range(1, RESULT_SLOTS + 1):
            @pl.when(i >= k)
            def _():
                wait_scatter(lax.rem(i + RESULT_SLOTS - k, RESULT_SLOTS))

        for k in range(GATHER_SLOTS - 1):
            wait_gather(lax.rem(i + k, GATHER_SLOTS))


def _expert_mlp(xn, row_tok, row_dst, block_expert, n_used, parity, next_expert, w_gate, w_up, w_down, n_out,
                rows=EXPERT_ROWS):
    d = xn.shape[1]
    n_rows = row_tok.shape[0]
    ff = w_gate.shape[-1]
    any_spec = pl.BlockSpec(memory_space=pl.ANY)
    return pl.pallas_call(
        functools.partial(_expert_kernel, rows=rows),
        grid_spec=pltpu.PrefetchScalarGridSpec(
            num_scalar_prefetch=6,
            grid=(n_rows // rows,),
            in_specs=[any_spec, any_spec, any_spec, any_spec],
            out_specs=any_spec,
            scratch_shapes=[pltpu.VMEM((GATHER_SLOTS, rows, d), F32), pltpu.VMEM((RESULT_SLOTS, rows, d), F32),
                            pltpu.SemaphoreType.DMA((GATHER_SLOTS,)), pltpu.SemaphoreType.DMA((RESULT_SLOTS,)),
                            pltpu.VMEM((2, d, ff), F32), pltpu.VMEM((2, d, ff), F32), pltpu.VMEM((2, ff, d), F32),
                            pltpu.SemaphoreType.DMA((2,)),
                            pltpu.VMEM((d, ff), BF16), pltpu.VMEM((d, ff), BF16), pltpu.VMEM((ff, d), BF16)]),
        out_shape=jax.ShapeDtypeStruct((n_out, d), F32),
        compiler_params=_params("arbitrary"),
        name="moe_experts",
    )(block_expert, n_used, row_tok, row_dst, parity, next_expert, xn, w_gate, w_up, w_down)


def _combine_kernel(*refs, with_norm):
    if with_norm:
        h_ref, meta_ref, y0_ref, y1_ref, g_ref, o_ref, hn_ref = refs
    else:
        h_ref, meta_ref, y0_ref, y1_ref, o_ref = refs
    meta = meta_ref[...]
    out = h_ref[...] + (y0_ref[...] * meta[:, 2:3] + y1_ref[...] * meta[:, 3:4])
    o_ref[...] = out
    if with_norm:
        ms = jnp.mean(out * out, axis=-1, keepdims=True)
        hn_ref[...] = (out * lax.rsqrt(ms + NORM_EPS) * g_ref[...]).astype(hn_ref.dtype)


def _combine(h, meta, y2, next_gain=None, tm=512):
    n, d = h.shape
    tm = min(tm, n)
    with_norm = next_gain is not None
    row_spec = pl.BlockSpec((tm, d), lambda i: (i, 0))
    in_specs = [row_spec, pl.BlockSpec((tm, LANES), lambda i: (i, 0)), row_spec,
                pl.BlockSpec((tm, d), lambda i: (n // tm + i, 0))]
    args = [h, meta, y2, y2]
    out_specs, out_shape = [row_spec], [jax.ShapeDtypeStruct((n, d), F32)]
    if with_norm:
        in_specs.append(pl.BlockSpec((1, d), lambda i: (0, 0)))
        args.append(next_gain.reshape(1, d).astype(F32))
        out_specs.append(row_spec)
        out_shape.append(jax.ShapeDtypeStruct((n, d), BF16))
    outs = pl.pallas_call(
        functools.partial(_combine_kernel, with_norm=with_norm),
        grid=(n // tm,),
        in_specs=in_specs,
        out_specs=out_specs,
        out_shape=out_shape,
        compiler_params=_params("parallel"),
        name="moe_combine",
    )(*args)
    return (outs[0], outs[1]) if with_norm else (outs[0], None)


def _split_router_weights(w_group, w_expert):
    d = w_group.shape[0]
    w = jnp.concatenate([w_group.astype(F32), w_expert.astype(F32),
                         jnp.zeros((d, LANES - N_GROUPS - N_EXPERTS), F32)], axis=1)
    hi = w.astype(BF16)
    lo = (w - hi.astype(F32)).astype(BF16)
    return jnp.concatenate([hi, lo], axis=1)


def _hier_moe(h, ffn_gain, w_group, w_expert, w_gate, w_up, w_down, layer, next_gain=None):
    n, d = h.shape
    rows = EXPERT_ROWS
    xn, meta, cnt = _router(h, ffn_gain, _split_router_weights(w_group, w_expert))
    ids = jnp.arange(N_EXPERTS, dtype=jnp.int32)
    expert = meta[:, :TOP_K].astype(jnp.int32).reshape(-1)
    counts = cnt[0, N_GROUPS:N_GROUPS + N_EXPERTS].astype(jnp.int32)
    n_assign = n * TOP_K
    n_blocks = n_assign // rows + N_EXPERTS
    n_rows = n_blocks * rows
    order = jnp.argsort(expert, stable=True).astype(jnp.int32)
    start = jnp.cumsum(counts) - counts
    pad_counts = (counts + rows - 1) // rows * rows
    pad_end = jnp.cumsum(pad_counts)
    pad_start = pad_end - pad_counts
    n_used = (pad_end[-1] // rows).astype(jnp.int32).reshape(1)
    blk = jnp.minimum(jnp.arange(n_blocks, dtype=jnp.int32), n_used[0] - 1)
    block_expert = jnp.sum((pad_end[None, :] <= (blk * rows)[:, None]).astype(jnp.int32), axis=1)
    block_expert = jnp.minimum(block_expert, N_EXPERTS - 1)
    b_idx = jnp.arange(n_blocks, dtype=jnp.int32)
    first = b_idx * rows - pad_start[block_expert]
    src = jnp.clip(start[block_expert] + first, 0, n_assign)
    j = jnp.arange(rows, dtype=jnp.int32)[None, :]
    order_padded = jnp.concatenate([order, jnp.zeros((rows,), jnp.int32)])
    choice = order_padded[src[:, None] + j]
    real = (b_idx < n_used[0])[:, None] & (first[:, None] + j < counts[block_expert][:, None])
    spare = TOP_K * n + (b_idx % RESULT_SLOTS)[:, None] * rows + j
    row_tok = jnp.where(real, choice // TOP_K, 0).reshape(-1)
    row_dst = jnp.where(real, (choice % TOP_K) * n + choice // TOP_K, spare).reshape(-1)
    nonempty = counts > 0
    position = jnp.cumsum(nonempty.astype(jnp.int32)) - 1
    later = jnp.where(nonempty[None, :] & (ids[None, :] > ids[:, None]), ids[None, :], N_EXPERTS)
    following = jnp.min(later, axis=1)
    following = jnp.where(following < N_EXPERTS, following + layer * N_EXPERTS, -1)
    y2 = _expert_mlp(xn, row_tok, row_dst, block_expert + layer * N_EXPERTS, n_used,
                     position[block_expert] % 2, following[block_expert], w_gate, w_up, w_down,
                     TOP_K * n + RESULT_SLOTS * rows, rows)
    return _combine(h, meta, y2, next_gain)


def kernel(x, positions, mix_norm, ffn_norm, hgrn_w_in, hgrn_out_gain, hgrn_w_out, hgrn_lb_logits, diff_w_in, diff_q_gain, diff_k_gain, diff_lambda, diff_sub_gain, diff_w_out, moe_w_group, moe_w_expert, moe_w_gate, moe_w_up, moe_w_down):
    b, s, d = x.shape
    n = b * s
    depth = mix_norm.shape[0]
    lower_bounds = jnp.cumsum(jax.nn.softmax(hgrn_lb_logits.astype(F32), axis=0), axis=0)
    h = x.reshape(n, d)
    w_gate = moe_w_gate.reshape((-1,) + moe_w_gate.shape[2:])
    w_up = moe_w_up.reshape((-1,) + moe_w_up.shape[2:])
    w_down = moe_w_down.reshape((-1,) + moe_w_down.shape[2:])
    hn = _rmsnorm_bf16(h, mix_norm[0])
    for layer in range(depth):
        j = layer // N_MIXERS
        if layer % N_MIXERS == 0:
            w_heads = (hgrn_w_in[j].reshape(d, 4, HGRN_HEADS, HGRN_DIM).transpose(2, 0, 1, 3)
                       .reshape(HGRN_HEADS, d, 4 * HGRN_DIM).astype(BF16))
            o = _hgrn_mixer(hn.reshape(b, s, d), w_heads, lower_bounds[layer], hgrn_out_gain[j])
            h = _matmul_residual(o.reshape(n, d), hgrn_w_out[j].astype(BF16), h)
        else:
            lam_init = 0.8 - 0.6 * math.exp(-0.3 * layer)
            hd = DIFF_HEAD_DIM
            half = hd // 2
            inv_freq = ROPE_THETA ** (-jnp.arange(half, dtype=F32) / half)
            ang = positions.astype(F32).reshape(n, 1) * inv_freq[None, :]
            cosf = jnp.concatenate([jnp.cos(ang), jnp.cos(ang)], axis=1)
            sins = jnp.concatenate([-jnp.sin(ang), jnp.sin(ang)], axis=1)
            n_heads = 2 * DIFF_HEADS
            n_qk = 2 * n_heads * hd
            gains = jnp.concatenate([jnp.tile(diff_q_gain[j].astype(F32), n_heads) * (hd ** -0.5),
                                     jnp.tile(diff_k_gain[j].astype(F32), n_heads)]).reshape(1, -1)
            w_in = diff_w_in[j].astype(BF16)
            qk = _qk_proj(hn, w_in[:, :n_qk], gains, cosf, sins)
            v = _matmul_bf16(hn, w_in[:, n_qk:])
            o = _diff_attention(qk.reshape(b, s, -1), v.reshape(b, s, -1), diff_lambda[j], diff_sub_gain[j],
                                lam_init)
            h = _matmul_residual(o.reshape(n, d), diff_w_out[j].astype(BF16), h)
        next_gain = mix_norm[layer + 1] if layer + 1 < depth else None
        h, hn = _hier_moe(h, ffn_norm[layer], moe_w_group[layer], moe_w_expert[layer], w_gate, w_up, w_down, layer,
                          next_gain)
    return h.reshape(b, s, d)
```

```python
import functools
import math

import jax
import jax.numpy as jnp
import numpy as np
from jax import lax
from jax.experimental import pallas as pl
from jax.experimental.pallas import tpu as pltpu

F32 = jnp.float32
BF16 = jnp.bfloat16

NORM_EPS = 1e-6
N_MIXERS = 2
HGRN_HEADS = 16
HGRN_DIM = 128
HGRN_BLOCK = 128
HGRN_CHUNK = 512
HGRN_SUB = 16
DIFF_HEADS = 8
DIFF_HEAD_DIM = 128
ROPE_THETA = 10000.0
ATTN_TILE = 256
N_GROUPS = 8
EXPERTS_PER_GROUP = 8
N_EXPERTS = N_GROUPS * EXPERTS_PER_GROUP
TOP_K = 2
EXPERT_ROWS = 256
GATHER_SLOTS = 4
RESULT_SLOTS = 4
ROW_DMA_PRIORITY = 0
WEIGHT_DMA_PRIORITY = 1
LANES = 128
VMEM_LIMIT_BYTES = 56 * 1024 * 1024


def _params(*semantics):
    return pltpu.CompilerParams(dimension_semantics=semantics, vmem_limit_bytes=VMEM_LIMIT_BYTES)


def _norm_kernel(h_ref, g_ref, o_ref):
    x = h_ref[...]
    ms = jnp.mean(x * x, axis=-1, keepdims=True)
    o_ref[...] = (x * lax.rsqrt(ms + NORM_EPS) * g_ref[...]).astype(o_ref.dtype)


def _rmsnorm_bf16(h, gain, tm=512):
    n, d = h.shape
    return pl.pallas_call(
        _norm_kernel,
        grid=(n // tm,),
        in_specs=[pl.BlockSpec((tm, d), lambda i: (i, 0)), pl.BlockSpec((1, d), lambda i: (0, 0))],
        out_specs=pl.BlockSpec((tm, d), lambda i: (i, 0)),
        out_shape=jax.ShapeDtypeStruct((n, d), BF16),
        compiler_params=_params("parallel"),
        name="rmsnorm_bf16",
    )(h, gain.reshape(1, d).astype(F32))


def _mm_res_kernel(a_ref, w_ref, r_ref, o_ref):
    o_ref[...] = r_ref[...] + jnp.dot(a_ref[...], w_ref[...], preferred_element_type=F32)


def _matmul_residual(a, w, res, tm=1024, tn=1024):
    n, k = a.shape
    nout = w.shape[1]
    tm, tn = min(tm, n), min(tn, nout)
    return pl.pallas_call(
        _mm_res_kernel,
        grid=(n // tm, nout // tn),
        in_specs=[pl.BlockSpec((tm, k), lambda i, j: (i, 0)),
                  pl.BlockSpec((k, tn), lambda i, j: (0, j)),
                  pl.BlockSpec((tm, tn), lambda i, j: (i, j))],
        out_specs=pl.BlockSpec((tm, tn), lambda i, j: (i, j)),
        out_shape=jax.ShapeDtypeStruct((n, nout), F32),
        compiler_params=_params("parallel", "parallel"),
        name="matmul_residual",
    )(a, w, res)


def _sigmoid(x):
    return 1.0 / (1.0 + jnp.exp(-x))


def _ref_rows(a_ref, half, block):
    span = 2 * half
    parts = [jnp.broadcast_to(a_ref[g * span + half - 1:g * span + half, :], (span, a_ref.shape[1]))
             for g in range(block // span)]
    return parts[0] if len(parts) == 1 else jnp.concatenate(parts, axis=0)


def _cumsum_rows(x, row):
    shift = 1
    while shift < x.shape[0]:
        x = x + jnp.where(row >= shift, pltpu.roll(x, shift, 0), 0.0)
        shift *= 2
    return x


def _hgrn_kernel(hn_ref, w_ref, lb_ref, og_ref, o_ref, proj_a, proj_b, cum_a, cum_b, *, seq, block, chunk):
    dim = HGRN_DIM
    blocks_per_chunk = chunk // block
    n_chunks = seq // chunk
    lb = lb_ref[0]
    og = og_ref[...]
    t_idx = lax.broadcasted_iota(jnp.int32, (block, block), 0)
    s_idx = lax.broadcasted_iota(jnp.int32, (block, block), 1)
    causal = s_idx <= t_idx
    split = t_idx ^ s_idx
    row = lax.broadcasted_iota(jnp.int32, (block, dim), 0)
    halves = []
    half = block // 2
    while half >= HGRN_SUB:
        halves.append(half)
        half //= 2
    contract_last = (((1,), (1,)), ((), ()))
    contract_first = (((0,), (0,)), ((), ()))

    def project(c, buf):
        r0 = pl.multiple_of(c * chunk, chunk)
        buf[...] = jnp.dot(hn_ref[0, pl.ds(r0, chunk), :], w_ref[0], preferred_element_type=F32)

    def one_block(buf, cum_ref, j, r0, state_t):
        rows = slice(j * block, (j + 1) * block)
        pq = buf[rows, 0 * dim:1 * dim]
        pf = buf[rows, 1 * dim:2 * dim]
        v16 = buf[rows, 2 * dim:3 * dim].astype(BF16)
        pg = buf[rows, 3 * dim:4 * dim]
        q = pq * _sigmoid(pq)
        forget = lb + (1.0 - lb) * _sigmoid(pf)
        k = 1.0 - forget
        cum = _cumsum_rows(jnp.log(forget), row)
        cum_ref[j] = cum
        blk = cum_ref.at[j]
        total = blk[block - 1:block, :]
        scores = None
        for half in halves:
            e = jnp.exp(-jnp.abs(cum - _ref_rows(blk, half, block)))
            p = lax.dot_general((q * e).astype(BF16), (k * e).astype(BF16), contract_last,
                                preferred_element_type=F32)
            scores = p if scores is None else jnp.where(split >= 2 * half, scores, p)
        sub_ref = jnp.concatenate(
            [jnp.zeros((HGRN_SUB, dim), F32)]
            + [jnp.broadcast_to(blk[s * HGRN_SUB - 1:s * HGRN_SUB, :], (HGRN_SUB, dim))
               for s in range(1, block // HGRN_SUB)], axis=0)
        loc = cum - sub_ref
        p = lax.dot_general((q * jnp.exp(loc)).astype(BF16), (k * jnp.exp(-loc)).astype(BF16),
                            contract_last, preferred_element_type=F32)
        scores = p if scores is None else jnp.where(split >= HGRN_SUB, scores, p)
        scores = jnp.where(causal, scores, 0.0)
        o = jnp.dot(scores.astype(BF16), v16, preferred_element_type=F32)
        o = o + lax.dot_general((q * jnp.exp(cum)).astype(BF16), state_t.astype(BF16), contract_last,
                                preferred_element_type=F32)
        k_end = (k * jnp.exp(total - cum)).astype(BF16)
        state_t = state_t * jnp.exp(total) + lax.dot_general(v16, k_end, contract_first,
                                                              preferred_element_type=F32)
        ms = jnp.mean(o * o, axis=-1, keepdims=True)
        y = o * lax.rsqrt(ms + NORM_EPS) * og
        o_ref[0, pl.ds(r0 + j * block, block), :] = (y * (pg * _sigmoid(pg))).astype(o_ref.dtype)
        return state_t

    bufs = (proj_a, proj_b)
    cum_refs = (cum_a, cum_b)
    project(0, bufs[0])

    def chunk_pair(cp, state_t):
        for parity in range(2):
            c = 2 * cp + parity
            project(jnp.minimum(c + 1, n_chunks - 1), bufs[1 - parity])
            r0 = pl.multiple_of(c * chunk, chunk)
            for j in range(blocks_per_chunk):
                state_t = one_block(bufs[parity], cum_refs[parity], j, r0, state_t)
        return state_t

    state = lax.fori_loop(0, n_chunks // 2, chunk_pair, jnp.zeros((dim, dim), F32))
    if n_chunks % 2:
        r0 = (n_chunks - 1) * chunk
        for j in range(blocks_per_chunk):
            state = one_block(bufs[0], cum_refs[0], j, r0, state)


def _hgrn_mixer(hn, w_heads, lb, out_gain):
    b, s, d = hn.shape
    dim = HGRN_DIM
    block = min(HGRN_BLOCK, s)
    chunk = min(HGRN_CHUNK, s)
    return pl.pallas_call(
        functools.partial(_hgrn_kernel, seq=s, block=block, chunk=chunk),
        grid=(b, HGRN_HEADS),
        in_specs=[pl.BlockSpec((1, s, d), lambda i, h: (i, 0, 0)),
                  pl.BlockSpec((1, d, 4 * dim), lambda i, h: (h, 0, 0)),
                  pl.BlockSpec((1, 1, dim), lambda i, h: (h, 0, 0)),
                  pl.BlockSpec((1, dim), lambda i, h: (0, 0))],
        out_specs=pl.BlockSpec((1, s, dim), lambda i, h: (i, 0, h)),
        out_shape=jax.ShapeDtypeStruct((b, s, d), BF16),
        scratch_shapes=[pltpu.VMEM((chunk, 4 * dim), F32), pltpu.VMEM((chunk, 4 * dim), F32),
                        pltpu.VMEM((chunk // block, block, dim), F32),
                        pltpu.VMEM((chunk // block, block, dim), F32)],
        compiler_params=_params("parallel", "arbitrary"),
        name="hgrn_mixer",
    )(hn, w_heads, lb.reshape(HGRN_HEADS, 1, dim), out_gain.reshape(1, dim).astype(F32))


def _qk_kernel(a_ref, w_ref, g_ref, cos_ref, sin_ref, o_ref, *, tn):
    hd = DIFF_HEAD_DIM
    a = a_ref[...]
    cosf = cos_ref[...]
    sins = sin_ref[...]
    starts = list(range(0, tn, 2 * hd))
    nxt = jnp.dot(a, w_ref[:, 0:2 * hd], preferred_element_type=F32)
    for n_chunk, c0 in enumerate(starts):
        acc = nxt
        if n_chunk + 1 < len(starts):
            c1 = starts[n_chunk + 1]
            nxt = jnp.dot(a, w_ref[:, c1:c1 + 2 * hd], preferred_element_type=F32)
        for hh in range(2):
            x = acc[:, hh * hd:(hh + 1) * hd]
            sl = slice(c0 + hh * hd, c0 + (hh + 1) * hd)
            ms = jnp.mean(x * x, axis=-1, keepdims=True)
            y = x * lax.rsqrt(ms + NORM_EPS) * g_ref[:, sl]
            rot = pltpu.roll(y, hd // 2, 1)
            o_ref[:, sl] = (y * cosf + rot * sins).astype(o_ref.dtype)


def _qk_proj(hn, w, gains, cosf, sins, tm=256, tn=1024):
    n, d = hn.shape
    nout = w.shape[1]
    tm = min(tm, n)
    return pl.pallas_call(
        functools.partial(_qk_kernel, tn=tn),
        grid=(nout // tn, n // tm),
        in_specs=[pl.BlockSpec((tm, d), lambda j, i: (i, 0)),
                  pl.BlockSpec((d, tn), lambda j, i: (0, j)),
                  pl.BlockSpec((1, tn), lambda j, i: (0, j)),
                  pl.BlockSpec((tm, DIFF_HEAD_DIM), lambda j, i: (i, 0)),
                  pl.BlockSpec((tm, DIFF_HEAD_DIM), lambda j, i: (i, 0))],
        out_specs=pl.BlockSpec((tm, tn), lambda j, i: (i, j)),
        out_shape=jax.ShapeDtypeStruct((n, nout), BF16),
        compiler_params=_params("parallel", "arbitrary"),
        name="qk_proj",
    )(hn, w, gains, cosf, sins)


def _mm_bf16_kernel(a_ref, w_ref, o_ref):
    o_ref[...] = jnp.dot(a_ref[...], w_ref[...], preferred_element_type=F32).astype(o_ref.dtype)


def _matmul_bf16(a, w, tm=1024, tn=1024):
    n, k = a.shape
    nout = w.shape[1]
    tm, tn = min(tm, n), min(tn, nout)
    return pl.pallas_call(
        _mm_bf16_kernel,
        grid=(n // tm, nout // tn),
        in_specs=[pl.BlockSpec((tm, k), lambda i, j: (i, 0)), pl.BlockSpec((k, tn), lambda i, j: (0, j))],
        out_specs=pl.BlockSpec((tm, tn), lambda i, j: (i, j)),
        out_shape=jax.ShapeDtypeStruct((n, nout), BF16),
        compiler_params=_params("parallel", "parallel"),
        name="matmul_bf16",
    )(a, w)


def _attn_kernel(q_ref, k_ref, v_ref, lam_ref, sg_ref, o_ref, kt_ref, s_ref, m_ref, l_ref, acc_ref, *, seq, tile,
                 lam_init):
    hd = DIFF_HEAD_DIM
    lf = lam_ref[...]
    lam = (jnp.exp(jnp.sum(lf[0:1] * lf[1:2], axis=-1, keepdims=True))
           - jnp.exp(jnp.sum(lf[2:3] * lf[3:4], axis=-1, keepdims=True)) + lam_init)
    offset = {w: (lax.broadcasted_iota(jnp.int32, (tile, w), 1) - lax.broadcasted_iota(jnp.int32, (tile, w), 0))
              for w in (tile, 2 * tile)}

    for c in range(2):
        for t in range(seq // tile):
            kt = k_ref[0, t * tile:(t + 1) * tile, c * hd:(c + 1) * hd].astype(F32).T
            kt_ref[c, :, t * tile:(t + 1) * tile] = kt.astype(BF16)

    def lane_fold(x, op):
        out = x[:, :LANES]
        for t in range(1, x.shape[1] // LANES):
            out = op(out, x[:, t * LANES:(t + 1) * LANES])
        return out

    def qblock(i, carry):
        q0 = pl.multiple_of(i * tile, tile)
        odd = lax.rem(i, 2)

        def score(k0, width):
            visible = offset[width] <= q0 - k0
            for c in range(2):
                q = q_ref[0, pl.ds(q0, tile), c * hd:(c + 1) * hd]
                s = jnp.dot(q, kt_ref[c, :, pl.ds(k0, width)], preferred_element_type=F32)
                s = jnp.where(visible, s, -jnp.inf)
                s_ref[c, :, pl.ds(k0, width)] = s
                m_ref[c] = jnp.maximum(m_ref[c], lane_fold(s, jnp.maximum))

        def value(k0, width):
            vj = v_ref[0, pl.ds(k0, width), :]
            for c in range(2):
                m = m_ref[c]
                e = jnp.exp(s_ref[c, :, pl.ds(k0, width)] - jnp.concatenate([m] * (width // LANES), axis=1))
                l_ref[c] = l_ref[c] + lane_fold(e, jnp.add)
                acc_ref[c] = acc_ref[c] + jnp.dot(e.astype(BF16), vj, preferred_element_type=F32)

        for c in range(2):
            m_ref[c] = jnp.full((tile, LANES), -jnp.inf, F32)

        @pl.when(odd == 0)
        def _():
            score(0, tile)

        def score_pair(p, c2):
            score(pl.multiple_of((1 - odd) * tile + p * 2 * tile, tile), 2 * tile)
            return c2

        lax.fori_loop(0, (i + 1) // 2, score_pair, 0)
        for c in range(2):
            m_ref[c] = jnp.broadcast_to(jnp.max(m_ref[c], axis=-1, keepdims=True), (tile, LANES))
            l_ref[c] = jnp.zeros((tile, LANES), F32)
            acc_ref[c] = jnp.zeros((tile, 2 * hd), F32)

        @pl.when(odd == 0)
        def _():
            value(0, tile)

        def value_pair(p, c2):
            value(pl.multiple_of((1 - odd) * tile + p * 2 * tile, tile), 2 * tile)
            return c2

        lax.fori_loop(0, (i + 1) // 2, value_pair, 0)
        inv = [1.0 / jnp.sum(l_ref[c], axis=-1, keepdims=True) for c in range(2)]
        o = acc_ref[0] * inv[0] - lam * (acc_ref[1] * inv[1])
        ms = jnp.mean(o * o, axis=-1, keepdims=True)
        y = o * lax.rsqrt(ms + NORM_EPS) * sg_ref[...]
        o_ref[0, pl.ds(q0, tile), :] = (y * (1.0 - lam_init)).astype(o_ref.dtype)
        return carry

    lax.fori_loop(0, seq // tile, qblock, 0)


def _diff_attention(qk, v, lam, sub_gain, lam_init):
    b, s, _ = qk.shape
    hd = DIFF_HEAD_DIM
    tile = min(ATTN_TILE, s)
    nh = DIFF_HEADS
    return pl.pallas_call(
        functools.partial(_attn_kernel, seq=s, tile=tile, lam_init=lam_init),
        grid=(b, nh),
        in_specs=[pl.BlockSpec((1, s, 2 * hd), lambda i, h: (i, 0, h)),
                  pl.BlockSpec((1, s, 2 * hd), lambda i, h: (i, 0, nh + h)),
                  pl.BlockSpec((1, s, 2 * hd), lambda i, h: (i, 0, h)),
                  pl.BlockSpec((4, hd), lambda i, h: (0, 0)),
                  pl.BlockSpec((1, 2 * hd), lambda i, h: (0, 0))],
        out_specs=pl.BlockSpec((1, s, 2 * hd), lambda i, h: (i, 0, h)),
        out_shape=jax.ShapeDtypeStruct((b, s, nh * 2 * hd), BF16),
        scratch_shapes=[pltpu.VMEM((2, hd, s), BF16), pltpu.VMEM((2, tile, s), F32), pltpu.VMEM((2, tile, LANES), F32),
                        pltpu.VMEM((2, tile, LANES), F32), pltpu.VMEM((2, tile, 2 * hd), F32)],
        compiler_params=_params("parallel", "parallel"),
        name="diff_attention",
    )(qk, qk, v, lam.astype(F32), sub_gain.reshape(1, 2 * hd).astype(F32))


def _router_kernel(h_ref, g_ref, w_ref, xn_ref, meta_ref, cnt_ref):
    x = h_ref[...]
    ms = jnp.mean(x * x, axis=-1, keepdims=True)
    xn = x * lax.rsqrt(ms + NORM_EPS) * g_ref[...]
    xn_ref[...] = xn
    x_hi = xn.astype(BF16)
    x_lo = (xn - x_hi.astype(F32)).astype(BF16)
    both = jnp.dot(x_hi, w_ref[...], preferred_element_type=F32)
    lg = both[:, :LANES] + both[:, LANES:] + jnp.dot(x_lo, w_ref[:, :LANES], preferred_element_type=F32)
    lane = lax.broadcasted_iota(jnp.int32, lg.shape, 1)
    is_group = lane < N_GROUPS
    gmax = jnp.max(jnp.where(is_group, lg, -jnp.inf), axis=-1, keepdims=True)
    ge = jnp.where(is_group, jnp.exp(lg - gmax), 0.0)
    gp = ge / jnp.sum(ge, axis=-1, keepdims=True)
    g_top_p = jnp.max(jnp.where(is_group, gp, -1.0), axis=-1, keepdims=True)
    g_top = jnp.min(jnp.where(is_group & (gp == g_top_p), lane, LANES), axis=-1, keepdims=True)
    first = N_GROUPS + g_top * EXPERTS_PER_GROUP
    in_group = (lane >= first) & (lane < first + EXPERTS_PER_GROUP)
    emax = jnp.max(jnp.where(in_group, lg, -jnp.inf), axis=-1, keepdims=True)
    ee = jnp.where(in_group, jnp.exp(lg - emax), 0.0)
    ep = ee / jnp.sum(ee, axis=-1, keepdims=True)
    p1 = jnp.max(jnp.where(in_group, ep, -1.0), axis=-1, keepdims=True)
    i1 = jnp.min(jnp.where(in_group & (ep == p1), lane, LANES), axis=-1, keepdims=True)
    rest = in_group & (lane != i1)
    p2 = jnp.max(jnp.where(rest, ep, -1.0), axis=-1, keepdims=True)
    i2 = jnp.min(jnp.where(rest & (ep == p2), lane, LANES), axis=-1, keepdims=True)
    denom = p1 + p2
    gate1 = g_top_p * p1 / denom
    gate2 = g_top_p * p2 / denom
    e1 = (i1 - N_GROUPS).astype(F32)
    e2 = (i2 - N_GROUPS).astype(F32)

    @pl.when(pl.program_id(0) == 0)
    def _():
        cnt_ref[...] = jnp.zeros_like(cnt_ref)

    chosen = jnp.where((lane == i1) | (lane == i2), 1.0, 0.0)
    cnt_ref[...] = cnt_ref[...] + jnp.sum(chosen, axis=0, keepdims=True)
    meta_ref[...] = jnp.where(lane == 0, e1, jnp.where(lane == 1, e2, jnp.where(lane == 2, gate1,
                              jnp.where(lane == 3, gate2, 0.0))))


def _router(h, gain, w_split, tm=256):
    n, d = h.shape
    tm = min(tm, n)
    return pl.pallas_call(
        _router_kernel,
        grid=(n // tm,),
        in_specs=[pl.BlockSpec((tm, d), lambda i: (i, 0)),
                  pl.BlockSpec((1, d), lambda i: (0, 0)),
                  pl.BlockSpec((d, 2 * LANES), lambda i: (0, 0))],
        out_specs=[pl.BlockSpec((tm, d), lambda i: (i, 0)), pl.BlockSpec((tm, LANES), lambda i: (i, 0)),
                   pl.BlockSpec((1, LANES), lambda i: (0, 0))],
        out_shape=[jax.ShapeDtypeStruct((n, d), F32), jax.ShapeDtypeStruct((n, LANES), F32),
                   jax.ShapeDtypeStruct((1, LANES), F32)],
        compiler_params=_params("arbitrary"),
        name="moe_router",
    )(h, gain.reshape(1, d).astype(F32), w_split)


def _row_copy(src_ref, dst_ref, src_row, dst_row, sem):
    return pltpu.make_async_copy(src_ref.at[pl.ds(src_row, 1)], dst_ref.at[pl.ds(dst_row, 1)], sem)


def _expert_kernel(be_ref, nb_ref, tok_ref, dst_ref, par_ref, nxt_ref,
                   xn_hbm, wg_hbm, wu_hbm, wd_hbm, y_hbm,
                   xbuf, ybuf, xsem, ysem, wg32, wu32, wd32, wsem, wg16, wu16, wd16, *, rows):
    i = pl.program_id(0)
    n_used = nb_ref[0]
    slot = lax.rem(i, RESULT_SLOTS)
    prev = lax.rem(i + RESULT_SLOTS - 1, RESULT_SLOTS)
    xslot = lax.rem(i, GATHER_SLOTS)
    stages = ((wg_hbm, wg32), (wu_hbm, wu32), (wd_hbm, wd32))

    def weights(expert, s):
        return [pltpu.make_async_copy(hbm.at[expert], stage.at[s], wsem.at[s]) for hbm, stage in stages]

    def gather(block, into):
        for r in range(rows):
            _row_copy(xn_hbm, xbuf.at[into], tok_ref[block * rows + r], r, xsem.at[into]).start(ROW_DMA_PRIORITY)

    def scatter(block, frm):
        for r in range(rows):
            _row_copy(ybuf.at[frm], y_hbm, r, dst_ref[block * rows + r], ysem.at[frm]).start(ROW_DMA_PRIORITY)

    def wait_gather(s):
        pltpu.make_async_copy(xn_hbm.at[pl.ds(0, rows)], xbuf.at[s], xsem.at[s]).wait()

    def wait_scatter(s):
        pltpu.make_async_copy(ybuf.at[s], y_hbm.at[pl.ds(0, rows)], ysem.at[s]).wait()

    def ahead(k):
        return jnp.minimum(i + k, n_used - 1)

    def mlp():
        x = xbuf[xslot].astype(BF16)
        g = jnp.dot(x, wg16[...], preferred_element_type=F32)
        u = jnp.dot(x, wu16[...], preferred_element_type=F32)
        hmid = (g * _sigmoid(g)) * u
        ybuf[slot] = jnp.dot(hmid.astype(BF16), wd16[...], preferred_element_type=F32)

    @pl.when(i == 0)
    def _():
        for c in weights(be_ref[0], 0):
            c.start(WEIGHT_DMA_PRIORITY)
        for k in range(GATHER_SLOTS - 1):
            gather(ahead(k), k)
        ybuf[1] = jnp.zeros((rows, ybuf.shape[2]), F32)
        n_real = y_hbm.shape[0] - RESULT_SLOTS * rows
        clears = [pltpu.make_async_copy(ybuf.at[1], y_hbm.at[pl.ds(n_real + p * rows, rows)], ysem.at[1])
                  for p in range(RESULT_SLOTS)]
        for c in clears:
            c.start()
        for c in clears:
            c.wait()

    new_expert = jnp.logical_or(i == 0, be_ref[i] != be_ref[jnp.maximum(i - 1, 0)])

    @pl.when(jnp.logical_and(i < n_used, new_expert))
    def _():
        s = par_ref[i]
        for c in weights(0, s):
            c.wait()

        @pl.when(nxt_ref[i] >= 0)
        def _():
            for c in weights(nxt_ref[i], 1 - s):
                c.start(WEIGHT_DMA_PRIORITY)

        wg16[...] = wg32[s].astype(BF16)
        wu16[...] = wu32[s].astype(BF16)
        wd16[...] = wd32[s].astype(BF16)

    @pl.when(i == 0)
    def _():
        wait_gather(xslot)
        gather(ahead(GATHER_SLOTS - 1), lax.rem(i + GATHER_SLOTS - 1, GATHER_SLOTS))
        mlp()

    @pl.when(jnp.logical_and(i >= 1, i < n_used))
    def _():
        @pl.when(i >= RESULT_SLOTS)
        def _():
            wait_scatter(slot)

        wait_gather(xslot)
        gather(ahead(GATHER_SLOTS - 1), lax.rem(i + GATHER_SLOTS - 1, GATHER_SLOTS))
        scatter(i - 1, prev)
        mlp()

    @pl.when(i == n_used)
    def _():
        scatter(i - 1, prev)
        for k in range(1, RESULT_SLOTS + 1):
            @pl.when(i >= k)
            def _():
                wait_scatter(lax.rem(i + RESULT_SLOTS - k, RESULT_SLOTS))

        for k in range(GATHER_SLOTS - 1):
            wait_gather(lax.rem(i + k, GATHER_SLOTS))


def _expert_mlp(xn, row_tok, row_dst, block_expert, n_used, parity, next_expert, w_gate, w_up, w_down, n_out,
                rows=EXPERT_ROWS):
    d = xn.shape[1]
    n_rows = row_tok.shape[0]
    ff = w_gate.shape[-1]
    any_spec = pl.BlockSpec(memory_space=pl.ANY)
    return pl.pallas_call(
        functools.partial(_expert_kernel, rows=rows),
        grid_spec=pltpu.PrefetchScalarGridSpec(
            num_scalar_prefetch=6,
            grid=(n_rows // rows,),
            in_specs=[any_spec, any_spec, any_spec, any_spec],
            out_specs=any_spec,
            scratch_shapes=[pltpu.VMEM((GATHER_SLOTS, rows, d), F32), pltpu.VMEM((RESULT_SLOTS, rows, d), F32),
                            pltpu.SemaphoreType.DMA((GATHER_SLOTS,)), pltpu.SemaphoreType.DMA((RESULT_SLOTS,)),
                            pltpu.VMEM((2, d, ff), F32), pltpu.VMEM((2, d, ff), F32), pltpu.VMEM((2, ff, d), F32),
                            pltpu.SemaphoreType.DMA((2,)),
                            pltpu.VMEM((d, ff), BF16), pltpu.VMEM((d, ff), BF16), pltpu.VMEM((ff, d), BF16)]),
        out_shape=jax.ShapeDtypeStruct((n_out, d), F32),
        compiler_params=_params("arbitrary"),
        name="moe_experts",
    )(block_expert, n_used, row_tok, row_dst, parity, next_expert, xn, w_gate, w_up, w_down)


def _combine_kernel(*refs, with_norm):
    if with_norm:
        h_ref, meta_ref, y0_ref, y1_ref, g_ref, o_ref, hn_ref = refs
    else:
        h_ref, meta_ref, y0_ref, y1_ref, o_ref = refs
    meta = meta_ref[...]
    out = h_ref[...] + (y0_ref[...] * meta[:, 2:3] + y1_ref[...] * meta[:, 3:4])
    o_ref[...] = out
    if with_norm:
        ms = jnp.mean(out * out, axis=-1, keepdims=True)
        hn_ref[...] = (out * lax.rsqrt(ms + NORM_EPS) * g_ref[...]).astype(hn_ref.dtype)


def _combine(h, meta, y2, next_gain=None, tm=512):
    n, d = h.shape
    tm = min(tm, n)
    with_norm = next_gain is not None
    row_spec = pl.BlockSpec((tm, d), lambda i: (i, 0))
    in_specs = [row_spec, pl.BlockSpec((tm, LANES), lambda i: (i, 0)), row_spec,
                pl.BlockSpec((tm, d), lambda i: (n // tm + i, 0))]
    args = [h, meta, y2, y2]
    out_specs, out_shape = [row_spec], [jax.ShapeDtypeStruct((n, d), F32)]
    if with_norm:
        in_specs.append(pl.BlockSpec((1, d), lambda i: (0, 0)))
        args.append(next_gain.reshape(1, d).astype(F32))
        out_specs.append(row_spec)
        out_shape.append(jax.ShapeDtypeStruct((n, d), BF16))
    outs = pl.pallas_call(
        functools.partial(_combine_kernel, with_norm=with_norm),
        grid=(n // tm,),
        in_specs=in_specs,
        out_specs=out_specs,
        out_shape=out_shape,
        compiler_params=_params("parallel"),
        name="moe_combine",
    )(*args)
    return (outs[0], outs[1]) if with_norm else (outs[0], None)


def _split_router_weights(w_group, w_expert):
    d = w_group.shape[0]
    w = jnp.concatenate([w_group.astype(F32), w_expert.astype(F32),
                         jnp.zeros((d, LANES - N_GROUPS - N_EXPERTS), F32)], axis=1)
    hi = w.astype(BF16)
    lo = (w - hi.astype(F32)).astype(BF16)
    return jnp.concatenate([hi, lo], axis=1)


def _hier_moe(h, ffn_gain, w_group, w_expert, w_gate, w_up, w_down, layer, next_gain=None):
    n, d = h.shape
    rows = EXPERT_ROWS
    xn, meta, cnt = _router(h, ffn_gain, _split_router_weights(w_group, w_expert))
    ids = jnp.arange(N_EXPERTS, dtype=jnp.int32)
    expert = meta[:, :TOP_K].astype(jnp.int32).reshape(-1)
    counts = cnt[0, N_GROUPS:N_GROUPS + N_EXPERTS].astype(jnp.int32)
    n_assign = n * TOP_K
    n_blocks = n_assign // rows + N_EXPERTS
    n_rows = n_blocks * rows
    order = jnp.argsort(expert, stable=True).astype(jnp.int32)
    start = jnp.cumsum(counts) - counts
    pad_counts = (counts + rows - 1) // rows * rows
    pad_end = jnp.cumsum(pad_counts)
    pad_start = pad_end - pad_counts
    n_used = (pad_end[-1] // rows).astype(jnp.int32).reshape(1)
    blk = jnp.minimum(jnp.arange(n_blocks, dtype=jnp.int32), n_used[0] - 1)
    block_expert = jnp.sum((pad_end[None, :] <= (blk * rows)[:, None]).astype(jnp.int32), axis=1)
    block_expert = jnp.minimum(block_expert, N_EXPERTS - 1)
    b_idx = jnp.arange(n_blocks, dtype=jnp.int32)
    first = b_idx * rows - pad_start[block_expert]
    src = jnp.clip(start[block_expert] + first, 0, n_assign)
    j = jnp.arange(rows, dtype=jnp.int32)[None, :]
    order_padded = jnp.concatenate([order, jnp.zeros((rows,), jnp.int32)])
    choice = order_padded[src[:, None] + j]
    real = (b_idx < n_used[0])[:, None] & (first[:, None] + j < counts[block_expert][:, None])
    spare = TOP_K * n + (b_idx % RESULT_SLOTS)[:, None] * rows + j
    row_tok = jnp.where(real, choice // TOP_K, 0).reshape(-1)
    row_dst = jnp.where(real, (choice % TOP_K) * n + choice // TOP_K, spare).reshape(-1)
    nonempty = counts > 0
    position = jnp.cumsum(nonempty.astype(jnp.int32)) - 1
    later = jnp.where(nonempty[None, :] & (ids[None, :] > ids[:, None]), ids[None, :], N_EXPERTS)
    following = jnp.min(later, axis=1)
    following = jnp.where(following < N_EXPERTS, following + layer * N_EXPERTS, -1)
    y2 = _expert_mlp(xn, row_tok, row_dst, block_expert + layer * N_EXPERTS, n_used,
                     position[block_expert] % 2, following[block_expert], w_gate, w_up, w_down,
                     TOP_K * n + RESULT_SLOTS * rows, rows)
    return _combine(h, meta, y2, next_gain)


def kernel(x, positions, mix_norm, ffn_norm, hgrn_w_in, hgrn_out_gain, hgrn_w_out, hgrn_lb_logits, diff_w_in, diff_q_gain, diff_k_gain, diff_lambda, diff_sub_gain, diff_w_out, moe_w_group, moe_w_expert, moe_w_gate, moe_w_up, moe_w_down):
    b, s, d = x.shape
    n = b * s
    depth = mix_norm.shape[0]
    lower_bounds = jnp.cumsum(jax.nn.softmax(hgrn_lb_logits.astype(F32), axis=0), axis=0)
    h = x.reshape(n, d)
    w_gate = moe_w_gate.reshape((-1,) + moe_w_gate.shape[2:])
    w_up = moe_w_up.reshape((-1,) + moe_w_up.shape[2:])
    w_down = moe_w_down.reshape((-1,) + moe_w_down.shape[2:])
    hn = _rmsnorm_bf16(h, mix_norm[0])
    for layer in range(depth):
        j = layer // N_MIXERS
        if layer % N_MIXERS == 0:
            w_heads = (hgrn_w_in[j].reshape(d, 4, HGRN_HEADS, HGRN_DIM).transpose(2, 0, 1, 3)
                       .reshape(HGRN_HEADS, d, 4 * HGRN_DIM).astype(BF16))
            o = _hgrn_mixer(hn.reshape(b, s, d), w_heads, lower_bounds[layer], hgrn_out_gain[j])
            h = _matmul_residual(o.reshape(n, d), hgrn_w_out[j].astype(BF16), h)
        else:
            lam_init = 0.8 - 0.6 * math.exp(-0.3 * layer)
            hd = DIFF_HEAD_DIM
            half = hd // 2
            inv_freq = ROPE_THETA ** (-jnp.arange(half, dtype=F32) / half)
            ang = positions.astype(F32).reshape(n, 1) * inv_freq[None, :]
            cosf = jnp.concatenate([jnp.cos(ang), jnp.cos(ang)], axis=1)
            sins = jnp.concatenate([-jnp.sin(ang), jnp.sin(ang)], axis=1)
            n_heads = 2 * DIFF_HEADS
            n_qk = 2 * n_heads * hd
            gains = jnp.concatenate([jnp.tile(diff_q_gain[j].astype(F32), n_heads) * (hd ** -0.5),
                                     jnp.tile(diff_k_gain[j].astype(F32), n_heads)]).reshape(1, -1)
            w_in = diff_w_in[j].astype(BF16)
            qk = _qk_proj(hn, w_in[:, :n_qk], gains, cosf, sins)
            v = _matmul_bf16(hn, w_in[:, n_qk:])
            o = _diff_attention(qk.reshape(b, s, -1), v.reshape(b, s, -1), diff_lambda[j], diff_sub_gain[j],
                                lam_init)
            h = _matmul_residual(o.reshape(n, d), diff_w_out[j].astype(BF16), h)
        next_gain = mix_norm[layer + 1] if layer + 1 < depth else None
        h, hn = _hier_moe(h, ffn_norm[layer], moe_w_group[layer], moe_w_expert[layer], w_gate, w_up, w_down, layer,
                          next_gain)
    return h.reshape(b, s, d)
```

```python
import functools
import math

import jax
import jax.numpy as jnp
import numpy as np
from jax import lax
from jax.experimental import pallas as pl
from jax.experimental.pallas import tpu as pltpu

F32 = jnp.float32
BF16 = jnp.bfloat16

NORM_EPS = 1e-6
N_MIXERS = 2
HGRN_HEADS = 16
HGRN_DIM = 128
HGRN_BLOCK = 128
HGRN_CHUNK = 512
HGRN_SUB = 16
DIFF_HEADS = 8
DIFF_HEAD_DIM = 128
ROPE_THETA = 10000.0
ATTN_TILE = 256
N_GROUPS = 8
EXPERTS_PER_GROUP = 8
N_EXPERTS = N_GROUPS * EXPERTS_PER_GROUP
TOP_K = 2
EXPERT_ROWS = 256
GATHER_SLOTS = 3
RESULT_SLOTS = 3
ROW_DMA_PRIORITY = 0
WEIGHT_DMA_PRIORITY = 1
LANES = 128
VMEM_LIMIT_BYTES = 56 * 1024 * 1024


def _params(*semantics):
    return pltpu.CompilerParams(dimension_semantics=semantics, vmem_limit_bytes=VMEM_LIMIT_BYTES)


def _norm_kernel(h_ref, g_ref, o_ref):
    x = h_ref[...]
    ms = jnp.mean(x * x, axis=-1, keepdims=True)
    o_ref[...] = (x * lax.rsqrt(ms + NORM_EPS) * g_ref[...]).astype(o_ref.dtype)


def _rmsnorm_bf16(h, gain, tm=512):
    n, d = h.shape
    return pl.pallas_call(
        _norm_kernel,
        grid=(n // tm,),
        in_specs=[pl.BlockSpec((tm, d), lambda i: (i, 0)), pl.BlockSpec((1, d), lambda i: (0, 0))],
        out_specs=pl.BlockSpec((tm, d), lambda i: (i, 0)),
        out_shape=jax.ShapeDtypeStruct((n, d), BF16),
        compiler_params=_params("parallel"),
        name="rmsnorm_bf16",
    )(h, gain.reshape(1, d).astype(F32))


def _mm_res_kernel(a_ref, w_ref, r_ref, o_ref):
    o_ref[...] = r_ref[...] + jnp.dot(a_ref[...], w_ref[...], preferred_element_type=F32)


def _matmul_residual(a, w, res, tm=1024, tn=1024):
    n, k = a.shape
    nout = w.shape[1]
    tm, tn = min(tm, n), min(tn, nout)
    return pl.pallas_call(
        _mm_res_kernel,
        grid=(n // tm, nout // tn),
        in_specs=[pl.BlockSpec((tm, k), lambda i, j: (i, 0)),
                  pl.BlockSpec((k, tn), lambda i, j: (0, j)),
                  pl.BlockSpec((tm, tn), lambda i, j: (i, j))],
        out_specs=pl.BlockSpec((tm, tn), lambda i, j: (i, j)),
        out_shape=jax.ShapeDtypeStruct((n, nout), F32),
        compiler_params=_params("parallel", "parallel"),
        name="matmul_residual",
    )(a, w, res)


def _sigmoid(x):
    return 1.0 / (1.0 + jnp.exp(-x))


def _ref_rows(a_ref, half, block):
    span = 2 * half
    parts = [jnp.broadcast_to(a_ref[g * span + half - 1:g * span + half, :], (span, a_ref.shape[1]))
             for g in range(block // span)]
    return parts[0] if len(parts) == 1 else jnp.concatenate(parts, axis=0)


def _cumsum_rows(x, row):
    shift = 1
    while shift < x.shape[0]:
        x = x + jnp.where(row >= shift, pltpu.roll(x, shift, 0), 0.0)
        shift *= 2
    return x


def _hgrn_kernel(hn_ref, w_ref, lb_ref, og_ref, o_ref, proj_a, proj_b, cum_a, cum_b, *, seq, block, chunk):
    dim = HGRN_DIM
    blocks_per_chunk = chunk // block
    n_chunks = seq // chunk
    lb = lb_ref[0]
    og = og_ref[...]
    t_idx = lax.broadcasted_iota(jnp.int32, (block, block), 0)
    s_idx = lax.broadcasted_iota(jnp.int32, (block, block), 1)
    causal = s_idx <= t_idx
    split = t_idx ^ s_idx
    row = lax.broadcasted_iota(jnp.int32, (block, dim), 0)
    halves = []
    half = block // 2
    while half >= HGRN_SUB:
        halves.append(half)
        half //= 2
    contract_last = (((1,), (1,)), ((), ()))
    contract_first = (((0,), (0,)), ((), ()))

    def project(c, buf):
        r0 = pl.multiple_of(c * chunk, chunk)
        buf[...] = jnp.dot(hn_ref[0, pl.ds(r0, chunk), :], w_ref[0], preferred_element_type=F32)

    def one_block(buf, cum_ref, j, r0, state_t):
        rows = slice(j * block, (j + 1) * block)
        pq = buf[rows, 0 * dim:1 * dim]
        pf = buf[rows, 1 * dim:2 * dim]
        v16 = buf[rows, 2 * dim:3 * dim].astype(BF16)
        pg = buf[rows, 3 * dim:4 * dim]
        q = pq * _sigmoid(pq)
        forget = lb + (1.0 - lb) * _sigmoid(pf)
        k = 1.0 - forget
        cum = _cumsum_rows(jnp.log(forget), row)
        cum_ref[j] = cum
        blk = cum_ref.at[j]
        total = blk[block - 1:block, :]
        scores = None
        for half in halves:
            e = jnp.exp(-jnp.abs(cum - _ref_rows(blk, half, block)))
            p = lax.dot_general((q * e).astype(BF16), (k * e).astype(BF16), contract_last,
                                preferred_element_type=F32)
            scores = p if scores is None else jnp.where(split >= 2 * half, scores, p)
        sub_ref = jnp.concatenate(
            [jnp.zeros((HGRN_SUB, dim), F32)]
            + [jnp.broadcast_to(blk[s * HGRN_SUB - 1:s * HGRN_SUB, :], (HGRN_SUB, dim))
               for s in range(1, block // HGRN_SUB)], axis=0)
        loc = cum - sub_ref
        p = lax.dot_general((q * jnp.exp(loc)).astype(BF16), (k * jnp.exp(-loc)).astype(BF16),
                            contract_last, preferred_element_type=F32)
        scores = p if scores is None else jnp.where(split >= HGRN_SUB, scores, p)
        scores = jnp.where(causal, scores, 0.0)
        o = jnp.dot(scores.astype(BF16), v16, preferred_element_type=F32)
        o = o + lax.dot_general((q * jnp.exp(cum)).astype(BF16), state_t.astype(BF16), contract_last,
                                preferred_element_type=F32)
        k_end = (k * jnp.exp(total - cum)).astype(BF16)
        state_t = state_t * jnp.exp(total) + lax.dot_general(v16, k_end, contract_first,
                                                              preferred_element_type=F32)
        ms = jnp.mean(o * o, axis=-1, keepdims=True)
        y = o * lax.rsqrt(ms + NORM_EPS) * og
        o_ref[0, pl.ds(r0 + j * block, block), :] = (y * (pg * _sigmoid(pg))).astype(o_ref.dtype)
        return state_t

    bufs = (proj_a, proj_b)
    cum_refs = (cum_a, cum_b)
    project(0, bufs[0])

    def chunk_pair(cp, state_t):
        for parity in range(2):
            c = 2 * cp + parity
            project(jnp.minimum(c + 1, n_chunks - 1), bufs[1 - parity])
            r0 = pl.multiple_of(c * chunk, chunk)
            for j in range(blocks_per_chunk):
                state_t = one_block(bufs[parity], cum_refs[parity], j, r0, state_t)
        return state_t

    state = lax.fori_loop(0, n_chunks // 2, chunk_pair, jnp.zeros((dim, dim), F32))
    if n_chunks % 2:
        r0 = (n_chunks - 1) * chunk
        for j in range(blocks_per_chunk):
            state = one_block(bufs[0], cum_refs[0], j, r0, state)


def _hgrn_mixer(hn, w_heads, lb, out_gain):
    b, s, d = hn.shape
    dim = HGRN_DIM
    block = min(HGRN_BLOCK, s)
    chunk = min(HGRN_CHUNK, s)
    return pl.pallas_call(
        functools.partial(_hgrn_kernel, seq=s, block=block, chunk=chunk),
        grid=(b, HGRN_HEADS),
        in_specs=[pl.BlockSpec((1, s, d), lambda i, h: (i, 0, 0)),
                  pl.BlockSpec((1, d, 4 * dim), lambda i, h: (h, 0, 0)),
                  pl.BlockSpec((1, 1, dim), lambda i, h: (h, 0, 0)),
                  pl.BlockSpec((1, dim), lambda i, h: (0, 0))],
        out_specs=pl.BlockSpec((1, s, dim), lambda i, h: (i, 0, h)),
        out_shape=jax.ShapeDtypeStruct((b, s, d), BF16),
        scratch_shapes=[pltpu.VMEM((chunk, 4 * dim), F32), pltpu.VMEM((chunk, 4 * dim), F32),
                        pltpu.VMEM((chunk // block, block, dim), F32),
                        pltpu.VMEM((chunk // block, block, dim), F32)],
        compiler_params=_params("parallel", "arbitrary"),
        name="hgrn_mixer",
    )(hn, w_heads, lb.reshape(HGRN_HEADS, 1, dim), out_gain.reshape(1, dim).astype(F32))


def _qk_kernel(a_ref, w_ref, g_ref, cos_ref, sin_ref, o_ref, *, tn):
    hd = DIFF_HEAD_DIM
    a = a_ref[...]
    cosf = cos_ref[...]
    sins = sin_ref[...]
    starts = list(range(0, tn, 2 * hd))
    nxt = jnp.dot(a, w_ref[:, 0:2 * hd], preferred_element_type=F32)
    for n_chunk, c0 in enumerate(starts):
        acc = nxt
        if n_chunk + 1 < len(starts):
            c1 = starts[n_chunk + 1]
            nxt = jnp.dot(a, w_ref[:, c1:c1 + 2 * hd], preferred_element_type=F32)
        for hh in range(2):
            x = acc[:, hh * hd:(hh + 1) * hd]
            sl = slice(c0 + hh * hd, c0 + (hh + 1) * hd)
            ms = jnp.mean(x * x, axis=-1, keepdims=True)
            y = x * lax.rsqrt(ms + NORM_EPS) * g_ref[:, sl]
            rot = pltpu.roll(y, hd // 2, 1)
            o_ref[:, sl] = (y * cosf + rot * sins).astype(o_ref.dtype)


def _qk_proj(hn, w, gains, cosf, sins, tm=256, tn=1024):
    n, d = hn.shape
    nout = w.shape[1]
    tm = min(tm, n)
    return pl.pallas_call(
        functools.partial(_qk_kernel, tn=tn),
        grid=(nout // tn, n // tm),
        in_specs=[pl.BlockSpec((tm, d), lambda j, i: (i, 0)),
                  pl.BlockSpec((d, tn), lambda j, i: (0, j)),
                  pl.BlockSpec((1, tn), lambda j, i: (0, j)),
                  pl.BlockSpec((tm, DIFF_HEAD_DIM), lambda j, i: (i, 0)),
                  pl.BlockSpec((tm, DIFF_HEAD_DIM), lambda j, i: (i, 0))],
        out_specs=pl.BlockSpec((tm, tn), lambda j, i: (i, j)),
        out_shape=jax.ShapeDtypeStruct((n, nout), BF16),
        compiler_params=_params("parallel", "arbitrary"),
        name="qk_proj",
    )(hn, w, gains, cosf, sins)


def _mm_bf16_kernel(a_ref, w_ref, o_ref):
    o_ref[...] = jnp.dot(a_ref[...], w_ref[...], preferred_element_type=F32).astype(o_ref.dtype)


def _matmul_bf16(a, w, tm=1024, tn=1024):
    n, k = a.shape
    nout = w.shape[1]
    tm, tn = min(tm, n), min(tn, nout)
    return pl.pallas_call(
        _mm_bf16_kernel,
        grid=(n // tm, nout // tn),
        in_specs=[pl.BlockSpec((tm, k), lambda i, j: (i, 0)), pl.BlockSpec((k, tn), lambda i, j: (0, j))],
        out_specs=pl.BlockSpec((tm, tn), lambda i, j: (i, j)),
        out_shape=jax.ShapeDtypeStruct((n, nout), BF16),
        compiler_params=_params("parallel", "parallel"),
        name="matmul_bf16",
    )(a, w)


def _attn_kernel(q_ref, k_ref, v_ref, lam_ref, sg_ref, o_ref, kt_ref, s_ref, m_ref, l_ref, acc_ref, *, seq, tile,
                 lam_init):
    hd = DIFF_HEAD_DIM
    lf = lam_ref[...]
    lam = (jnp.exp(jnp.sum(lf[0:1] * lf[1:2], axis=-1, keepdims=True))
           - jnp.exp(jnp.sum(lf[2:3] * lf[3:4], axis=-1, keepdims=True)) + lam_init)
    row = lax.broadcasted_iota(jnp.int32, (tile, tile), 0)
    col = lax.broadcasted_iota(jnp.int32, (tile, tile), 1)

    for c in range(2):
        for t in range(seq // tile):
            kt = k_ref[0, t * tile:(t + 1) * tile, c * hd:(c + 1) * hd].astype(F32).T
            kt_ref[c, :, t * tile:(t + 1) * tile] = kt.astype(BF16)

    def lane_fold(x, op):
        out = x[:, :LANES]
        for t in range(1, x.shape[1] // LANES):
            out = op(out, x[:, t * LANES:(t + 1) * LANES])
        return out

    def qblock(i, carry):
        q0 = pl.multiple_of(i * tile, tile)
        odd = lax.rem(i, 2)

        def score(k0, width):
            for c in range(2):
                q = q_ref[0, pl.ds(q0, tile), c * hd:(c + 1) * hd]
                s = jnp.dot(q, kt_ref[c, :, pl.ds(k0, width)], preferred_element_type=F32)
                s_ref[c, :, pl.ds(k0, width)] = s
                m_ref[c] = jnp.maximum(m_ref[c], lane_fold(s, jnp.maximum))

        def value(k0, width):
            vj = v_ref[0, pl.ds(k0, width), :]
            for c in range(2):
                m = m_ref[c]
                e = jnp.exp(s_ref[c, :, pl.ds(k0, width)] - jnp.concatenate([m] * (width // LANES), axis=1))
                l_ref[c] = l_ref[c] + lane_fold(e, jnp.add)
                acc_ref[c] = acc_ref[c] + jnp.dot(e.astype(BF16), vj, preferred_element_type=F32)

        for c in range(2):
            m_ref[c] = jnp.full((tile, LANES), -jnp.inf, F32)

        @pl.when(odd == 1)
        def _():
            score(0, tile)

        def score_pair(p, c2):
            score(pl.multiple_of(odd * tile + p * 2 * tile, tile), 2 * tile)
            return c2

        lax.fori_loop(0, i // 2, score_pair, 0)
        for c in range(2):
            q = q_ref[0, pl.ds(q0, tile), c * hd:(c + 1) * hd]
            s = jnp.dot(q, kt_ref[c, :, pl.ds(q0, tile)], preferred_element_type=F32)
            s = jnp.where(col <= row, s, -jnp.inf)
            s_ref[c, :, pl.ds(q0, tile)] = s
            m_part = jnp.maximum(m_ref[c], lane_fold(s, jnp.maximum))
            m_ref[c] = jnp.broadcast_to(jnp.max(m_part, axis=-1, keepdims=True), (tile, LANES))
            l_ref[c] = jnp.zeros((tile, LANES), F32)
            acc_ref[c] = jnp.zeros((tile, 2 * hd), F32)

        @pl.when(odd == 0)
        def _():
            value(0, tile)

        def value_pair(p, c2):
            value(pl.multiple_of((1 - odd) * tile + p * 2 * tile, tile), 2 * tile)
            return c2

        lax.fori_loop(0, (i + 1) // 2, value_pair, 0)
        inv = [1.0 / jnp.sum(l_ref[c], axis=-1, keepdims=True) for c in range(2)]
        o = acc_ref[0] * inv[0] - lam * (acc_ref[1] * inv[1])
        ms = jnp.mean(o * o, axis=-1, keepdims=True)
        y = o * lax.rsqrt(ms + NORM_EPS) * sg_ref[...]
        o_ref[0, pl.ds(q0, tile), :] = (y * (1.0 - lam_init)).astype(o_ref.dtype)
        return carry

    lax.fori_loop(0, seq // tile, qblock, 0)


def _diff_attention(qk, v, lam, sub_gain, lam_init):
    b, s, _ = qk.shape
    hd = DIFF_HEAD_DIM
    tile = min(ATTN_TILE, s)
    nh = DIFF_HEADS
    return pl.pallas_call(
        functools.partial(_attn_kernel, seq=s, tile=tile, lam_init=lam_init),
        grid=(b, nh),
        in_specs=[pl.BlockSpec((1, s, 2 * hd), lambda i, h: (i, 0, h)),
                  pl.BlockSpec((1, s, 2 * hd), lambda i, h: (i, 0, nh + h)),
                  pl.BlockSpec((1, s, 2 * hd), lambda i, h: (i, 0, h)),
                  pl.BlockSpec((4, hd), lambda i, h: (0, 0)),
                  pl.BlockSpec((1, 2 * hd), lambda i, h: (0, 0))],
        out_specs=pl.BlockSpec((1, s, 2 * hd), lambda i, h: (i, 0, h)),
        out_shape=jax.ShapeDtypeStruct((b, s, nh * 2 * hd), BF16),
        scratch_shapes=[pltpu.VMEM((2, hd, s), BF16), pltpu.VMEM((2, tile, s), F32), pltpu.VMEM((2, tile, LANES), F32),
                        pltpu.VMEM((2, tile, LANES), F32), pltpu.VMEM((2, tile, 2 * hd), F32)],
        compiler_params=_params("parallel", "parallel"),
        name="diff_attention",
    )(qk, qk, v, lam.astype(F32), sub_gain.reshape(1, 2 * hd).astype(F32))


def _router_kernel(h_ref, g_ref, w_ref, xn_ref, meta_ref, cnt_ref):
    x = h_ref[...]
    ms = jnp.mean(x * x, axis=-1, keepdims=True)
    xn = x * lax.rsqrt(ms + NORM_EPS) * g_ref[...]
    xn_ref[...] = xn
    x_hi = xn.astype(BF16)
    x_lo = (xn - x_hi.astype(F32)).astype(BF16)
    both = jnp.dot(x_hi, w_ref[...], preferred_element_type=F32)
    lg = both[:, :LANES] + both[:, LANES:] + jnp.dot(x_lo, w_ref[:, :LANES], preferred_element_type=F32)
    lane = lax.broadcasted_iota(jnp.int32, lg.shape, 1)
    is_group = lane < N_GROUPS
    gmax = jnp.max(jnp.where(is_group, lg, -jnp.inf), axis=-1, keepdims=True)
    ge = jnp.where(is_group, jnp.exp(lg - gmax), 0.0)
    gp = ge / jnp.sum(ge, axis=-1, keepdims=True)
    g_top_p = jnp.max(jnp.where(is_group, gp, -1.0), axis=-1, keepdims=True)
    g_top = jnp.min(jnp.where(is_group & (gp == g_top_p), lane, LANES), axis=-1, keepdims=True)
    first = N_GROUPS + g_top * EXPERTS_PER_GROUP
    in_group = (lane >= first) & (lane < first + EXPERTS_PER_GROUP)
    emax = jnp.max(jnp.where(in_group, lg, -jnp.inf), axis=-1, keepdims=True)
    ee = jnp.where(in_group, jnp.exp(lg - emax), 0.0)
    ep = ee / jnp.sum(ee, axis=-1, keepdims=True)
    p1 = jnp.max(jnp.where(in_group, ep, -1.0), axis=-1, keepdims=True)
    i1 = jnp.min(jnp.where(in_group & (ep == p1), lane, LANES), axis=-1, keepdims=True)
    rest = in_group & (lane != i1)
    p2 = jnp.max(jnp.where(rest, ep, -1.0), axis=-1, keepdims=True)
    i2 = jnp.min(jnp.where(rest & (ep == p2), lane, LANES), axis=-1, keepdims=True)
    denom = p1 + p2
    gate1 = g_top_p * p1 / denom
    gate2 = g_top_p * p2 / denom
    e1 = (i1 - N_GROUPS).astype(F32)
    e2 = (i2 - N_GROUPS).astype(F32)

    @pl.when(pl.program_id(0) == 0)
    def _():
        cnt_ref[...] = jnp.zeros_like(cnt_ref)

    chosen = jnp.where((lane == i1) | (lane == i2), 1.0, 0.0)
    cnt_ref[...] = cnt_ref[...] + jnp.sum(chosen, axis=0, keepdims=True)
    meta_ref[...] = jnp.where(lane == 0, e1, jnp.where(lane == 1, e2, jnp.where(lane == 2, gate1,
                              jnp.where(lane == 3, gate2, 0.0))))


def _router(h, gain, w_split, tm=256):
    n, d = h.shape
    tm = min(tm, n)
    return pl.pallas_call(
        _router_kernel,
        grid=(n // tm,),
        in_specs=[pl.BlockSpec((tm, d), lambda i: (i, 0)),
                  pl.BlockSpec((1, d), lambda i: (0, 0)),
                  pl.BlockSpec((d, 2 * LANES), lambda i: (0, 0))],
        out_specs=[pl.BlockSpec((tm, d), lambda i: (i, 0)), pl.BlockSpec((tm, LANES), lambda i: (i, 0)),
                   pl.BlockSpec((1, LANES), lambda i: (0, 0))],
        out_shape=[jax.ShapeDtypeStruct((n, d), F32), jax.ShapeDtypeStruct((n, LANES), F32),
                   jax.ShapeDtypeStruct((1, LANES), F32)],
        compiler_params=_params("arbitrary"),
        name="moe_router",
    )(h, gain.reshape(1, d).astype(F32), w_split)


def _row_copy(src_ref, dst_ref, src_row, dst_row, sem):
    return pltpu.make_async_copy(src_ref.at[pl.ds(src_row, 1)], dst_ref.at[pl.ds(dst_row, 1)], sem)


def _expert_kernel(be_ref, nb_ref, tok_ref, dst_ref, par_ref, nxt_ref,
                   xn_hbm, wg_hbm, wu_hbm, wd_hbm, y_hbm,
                   xbuf, ybuf, xsem, ysem, wg32, wu32, wd32, wsem, wg16, wu16, wd16, *, rows):
    i = pl.program_id(0)
    n_used = nb_ref[0]
    slot = lax.rem(i, RESULT_SLOTS)
    prev = lax.rem(i + RESULT_SLOTS - 1, RESULT_SLOTS)
    xslot = lax.rem(i, GATHER_SLOTS)
    stages = ((wg_hbm, wg32), (wu_hbm, wu32), (wd_hbm, wd32))

    def weights(expert, s):
        return [pltpu.make_async_copy(hbm.at[expert], stage.at[s], wsem.at[s]) for hbm, stage in stages]

    def gather(block, into):
        for r in range(rows):
            _row_copy(xn_hbm, xbuf.at[into], tok_ref[block * rows + r], r, xsem.at[into]).start(ROW_DMA_PRIORITY)

    def scatter(block, frm):
        for r in range(rows):
            _row_copy(ybuf.at[frm], y_hbm, r, dst_ref[block * rows + r], ysem.at[frm]).start(r % 2)

    def wait_gather(s):
        pltpu.make_async_copy(xn_hbm.at[pl.ds(0, rows)], xbuf.at[s], xsem.at[s]).wait()

    def wait_scatter(s):
        pltpu.make_async_copy(ybuf.at[s], y_hbm.at[pl.ds(0, rows)], ysem.at[s]).wait()

    def ahead(k):
        return jnp.minimum(i + k, n_used - 1)

    def mlp():
        x = xbuf[xslot].astype(BF16)
        g = jnp.dot(x, wg16[...], preferred_element_type=F32)
        u = jnp.dot(x, wu16[...], preferred_element_type=F32)
        hmid = (g * _sigmoid(g)) * u
        ybuf[slot] = jnp.dot(hmid.astype(BF16), wd16[...], preferred_element_type=F32)

    @pl.when(i == 0)
    def _():
        for c in weights(be_ref[0], 0):
            c.start(WEIGHT_DMA_PRIORITY)
        for k in range(GATHER_SLOTS - 1):
            gather(ahead(k), k)
        ybuf[1] = jnp.zeros((rows, ybuf.shape[2]), F32)
        n_real = y_hbm.shape[0] - RESULT_SLOTS * rows
        clears = [pltpu.make_async_copy(ybuf.at[1], y_hbm.at[pl.ds(n_real + p * rows, rows)], ysem.at[1])
                  for p in range(RESULT_SLOTS)]
        for c in clears:
            c.start()
        for c in clears:
            c.wait()

    new_expert = jnp.logical_or(i == 0, be_ref[i] != be_ref[jnp.maximum(i - 1, 0)])

    @pl.when(jnp.logical_and(i < n_used, new_expert))
    def _():
        s = par_ref[i]
        for c in weights(0, s):
            c.wait()

        @pl.when(nxt_ref[i] >= 0)
        def _():
            for c in weights(nxt_ref[i], 1 - s):
                c.start(WEIGHT_DMA_PRIORITY)

        wg16[...] = wg32[s].astype(BF16)
        wu16[...] = wu32[s].astype(BF16)
        wd16[...] = wd32[s].astype(BF16)

    @pl.when(i == 0)
    def _():
        wait_gather(xslot)
        gather(ahead(GATHER_SLOTS - 1), lax.rem(i + GATHER_SLOTS - 1, GATHER_SLOTS))
        mlp()

    @pl.when(jnp.logical_and(i >= 1, i < n_used))
    def _():
        @pl.when(i >= RESULT_SLOTS)
        def _():
            wait_scatter(slot)

        wait_gather(xslot)
        gather(ahead(GATHER_SLOTS - 1), lax.rem(i + GATHER_SLOTS - 1, GATHER_SLOTS))
        scatter(i - 1, prev)
        mlp()

    @pl.when(i == n_used)
    def _():
        scatter(i - 1, prev)
        for k in range(1, RESULT_SLOTS + 1):
            @pl.when(i >= k)
            def _():
                wait_scatter(lax.rem(i + RESULT_SLOTS - k, RESULT_SLOTS))

        for k in range(GATHER_SLOTS - 1):
            wait_gather(lax.rem(i + k, GATHER_SLOTS))


def _expert_mlp(xn, row_tok, row_dst, block_expert, n_used, parity, next_expert, w_gate, w_up, w_down, n_out,
                rows=EXPERT_ROWS):
    d = xn.shape[1]
    n_rows = row_tok.shape[0]
    ff = w_gate.shape[-1]
    any_spec = pl.BlockSpec(memory_space=pl.ANY)
    return pl.pallas_call(
        functools.partial(_expert_kernel, rows=rows),
        grid_spec=pltpu.PrefetchScalarGridSpec(
            num_scalar_prefetch=6,
            grid=(n_rows // rows,),
            in_specs=[any_spec, any_spec, any_spec, any_spec],
            out_specs=any_spec,
            scratch_shapes=[pltpu.VMEM((GATHER_SLOTS, rows, d), F32), pltpu.VMEM((RESULT_SLOTS, rows, d), F32),
                            pltpu.SemaphoreType.DMA((GATHER_SLOTS,)), pltpu.SemaphoreType.DMA((RESULT_SLOTS,)),
                            pltpu.VMEM((2, d, ff), F32), pltpu.VMEM((2, d, ff), F32), pltpu.VMEM((2, ff, d), F32),
                            pltpu.SemaphoreType.DMA((2,)),
                            pltpu.VMEM((d, ff), BF16), pltpu.VMEM((d, ff), BF16), pltpu.VMEM((ff, d), BF16)]),
        out_shape=jax.ShapeDtypeStruct((n_out, d), F32),
        compiler_params=_params("arbitrary"),
        name="moe_experts",
    )(block_expert, n_used, row_tok, row_dst, parity, next_expert, xn, w_gate, w_up, w_down)


def _combine_kernel(*refs, with_norm):
    if with_norm:
        h_ref, meta_ref, y0_ref, y1_ref, g_ref, o_ref, hn_ref = refs
    else:
        h_ref, meta_ref, y0_ref, y1_ref, o_ref = refs
    meta = meta_ref[...]
    out = h_ref[...] + (y0_ref[...] * meta[:, 2:3] + y1_ref[...] * meta[:, 3:4])
    o_ref[...] = out
    if with_norm:
        ms = jnp.mean(out * out, axis=-1, keepdims=True)
        hn_ref[...] = (out * lax.rsqrt(ms + NORM_EPS) * g_ref[...]).astype(hn_ref.dtype)


def _combine(h, meta, y2, next_gain=None, tm=256):
    n, d = h.shape
    tm = min(tm, n)
    with_norm = next_gain is not None
    row_spec = pl.BlockSpec((tm, d), lambda i: (i, 0))
    in_specs = [row_spec, pl.BlockSpec((tm, LANES), lambda i: (i, 0)), row_spec,
                pl.BlockSpec((tm, d), lambda i: (n // tm + i, 0))]
    args = [h, meta, y2, y2]
    out_specs, out_shape = [row_spec], [jax.ShapeDtypeStruct((n, d), F32)]
    if with_norm:
        in_specs.append(pl.BlockSpec((1, d), lambda i: (0, 0)))
        args.append(next_gain.reshape(1, d).astype(F32))
        out_specs.append(row_spec)
        out_shape.append(jax.ShapeDtypeStruct((n, d), BF16))
    outs = pl.pallas_call(
        functools.partial(_combine_kernel, with_norm=with_norm),
        grid=(n // tm,),
        in_specs=in_specs,
        out_specs=out_specs,
        out_shape=out_shape,
        compiler_params=_params("parallel"),
        name="moe_combine",
    )(*args)
    return (outs[0], outs[1]) if with_norm else (outs[0], None)


def _split_router_weights(w_group, w_expert):
    d = w_group.shape[0]
    w = jnp.concatenate([w_group.astype(F32), w_expert.astype(F32),
                         jnp.zeros((d, LANES - N_GROUPS - N_EXPERTS), F32)], axis=1)
    hi = w.astype(BF16)
    lo = (w - hi.astype(F32)).astype(BF16)
    return jnp.concatenate([hi, lo], axis=1)


def _hier_moe(h, ffn_gain, w_group, w_expert, w_gate, w_up, w_down, layer, next_gain=None):
    n, d = h.shape
    rows = EXPERT_ROWS
    xn, meta, cnt = _router(h, ffn_gain, _split_router_weights(w_group, w_expert))
    ids = jnp.arange(N_EXPERTS, dtype=jnp.int32)
    expert = meta[:, :TOP_K].astype(jnp.int32).reshape(-1)
    counts = cnt[0, N_GROUPS:N_GROUPS + N_EXPERTS].astype(jnp.int32)
    n_assign = n * TOP_K
    n_blocks = n_assign // rows + N_EXPERTS
    n_rows = n_blocks * rows
    order = jnp.argsort(expert, stable=True).astype(jnp.int32)
    start = jnp.cumsum(counts) - counts
    pad_counts = (counts + rows - 1) // rows * rows
    pad_end = jnp.cumsum(pad_counts)
    pad_start = pad_end - pad_counts
    n_used = (pad_end[-1] // rows).astype(jnp.int32).reshape(1)
    blk = jnp.minimum(jnp.arange(n_blocks, dtype=jnp.int32), n_used[0] - 1)
    block_expert = jnp.sum((pad_end[None, :] <= (blk * rows)[:, None]).astype(jnp.int32), axis=1)
    block_expert = jnp.minimum(block_expert, N_EXPERTS - 1)
    b_idx = jnp.arange(n_blocks, dtype=jnp.int32)
    first = b_idx * rows - pad_start[block_expert]
    src = jnp.clip(start[block_expert] + first, 0, n_assign)
    j = jnp.arange(rows, dtype=jnp.int32)[None, :]
    order_padded = jnp.concatenate([order, jnp.zeros((rows,), jnp.int32)])
    choice = order_padded[src[:, None] + j]
    real = (b_idx < n_used[0])[:, None] & (first[:, None] + j < counts[block_expert][:, None])
    spare = TOP_K * n + (b_idx % RESULT_SLOTS)[:, None] * rows + j
    row_tok = jnp.where(real, choice // TOP_K, 0).reshape(-1)
    row_dst = jnp.where(real, (choice % TOP_K) * n + choice // TOP_K, spare).reshape(-1)
    nonempty = counts > 0
    position = jnp.cumsum(nonempty.astype(jnp.int32)) - 1
    later = jnp.where(nonempty[None, :] & (ids[None, :] > ids[:, None]), ids[None, :], N_EXPERTS)
    following = jnp.min(later, axis=1)
    following = jnp.where(following < N_EXPERTS, following + layer * N_EXPERTS, -1)
    y2 = _expert_mlp(xn, row_tok, row_dst, block_expert + layer * N_EXPERTS, n_used,
                     position[block_expert] % 2, following[block_expert], w_gate, w_up, w_down,
                     TOP_K * n + RESULT_SLOTS * rows, rows)
    return _combine(h, meta, y2, next_gain)


def kernel(x, positions, mix_norm, ffn_norm, hgrn_w_in, hgrn_out_gain, hgrn_w_out, hgrn_lb_logits, diff_w_in, diff_q_gain, diff_k_gain, diff_lambda, diff_sub_gain, diff_w_out, moe_w_group, moe_w_expert, moe_w_gate, moe_w_up, moe_w_down):
    b, s, d = x.shape
    n = b * s
    depth = mix_norm.shape[0]
    lower_bounds = jnp.cumsum(jax.nn.softmax(hgrn_lb_logits.astype(F32), axis=0), axis=0)
    h = x.reshape(n, d)
    w_gate = moe_w_gate.reshape((-1,) + moe_w_gate.shape[2:])
    w_up = moe_w_up.reshape((-1,) + moe_w_up.shape[2:])
    w_down = moe_w_down.reshape((-1,) + moe_w_down.shape[2:])
    hn = _rmsnorm_bf16(h, mix_norm[0])
    for layer in range(depth):
        j = layer // N_MIXERS
        if layer % N_MIXERS == 0:
            w_heads = (hgrn_w_in[j].reshape(d, 4, HGRN_HEADS, HGRN_DIM).transpose(2, 0, 1, 3)
                       .reshape(HGRN_HEADS, d, 4 * HGRN_DIM).astype(BF16))
            o = _hgrn_mixer(hn.reshape(b, s, d), w_heads, lower_bounds[layer], hgrn_out_gain[j])
            h = _matmul_residual(o.reshape(n, d), hgrn_w_out[j].astype(BF16), h)
        else:
            lam_init = 0.8 - 0.6 * math.exp(-0.3 * layer)
            hd = DIFF_HEAD_DIM
            half = hd // 2
            inv_freq = ROPE_THETA ** (-jnp.arange(half, dtype=F32) / half)
            ang = positions.astype(F32).reshape(n, 1) * inv_freq[None, :]
            cosf = jnp.concatenate([jnp.cos(ang), jnp.cos(ang)], axis=1)
            sins = jnp.concatenate([-jnp.sin(ang), jnp.sin(ang)], axis=1)
            n_heads = 2 * DIFF_HEADS
            n_qk = 2 * n_heads * hd
            gains = jnp.concatenate([jnp.tile(diff_q_gain[j].astype(F32), n_heads) * (hd ** -0.5),
                                     jnp.tile(diff_k_gain[j].astype(F32), n_heads)]).reshape(1, -1)
            w_in = diff_w_in[j].astype(BF16)
            qk = _qk_proj(hn, w_in[:, :n_qk], gains, cosf, sins)
            v = _matmul_bf16(hn, w_in[:, n_qk:])
            o = _diff_attention(qk.reshape(b, s, -1), v.reshape(b, s, -1), diff_lambda[j], diff_sub_gain[j],
                                lam_init)
            h = _matmul_residual(o.reshape(n, d), diff_w_out[j].astype(BF16), h)
        next_gain = mix_norm[layer + 1] if layer + 1 < depth else None
        h, hn = _hier_moe(h, ffn_norm[layer], moe_w_group[layer], moe_w_expert[layer], w_gate, w_up, w_down, layer,
                          next_gain)
    return h.reshape(b, s, d)
```
